```python
import math
import jax, jax.numpy as jnp
from jax import lax
import numpy as np

D_MODEL = 4096
BATCH = 4
SEQ = 2048
DEPTH = 4
DEC_BATCH = 32
DEC_SEQ = 8
PAST_LEN = 8192
PAGE_SIZE = 128

A_HEADS = D_MODEL // 512
A_DK = 128
A_DV = 128
A_QK = A_HEADS * A_DK
A_VAL = A_HEADS * A_DV
A_CONV = 4
A_CONV_CH = 2 * A_QK + A_VAL
DELTA_CHUNK = 64
B_HEADS = D_MODEL // 256
B_KV_HEADS = B_HEADS // 4
B_GROUP = B_HEADS // B_KV_HEADS
B_HD = 64
B_Q = B_HEADS * B_HD
B_KV = B_KV_HEADS * B_HD
WINDOW = 128
C_GROUPS = D_MODEL // 512
C_GC = 128
C_HALF = C_GROUPS * C_GC
C_CHUNK = 128
D_FF = 3 * D_MODEL
FFN_CONV = 3
PLE_DIM = 256
N_BRANCH = 3
EPS = 1e-6
IN_SIZES = (A_CONV_CH, A_VAL, A_HEADS, A_HEADS, B_Q, B_KV, B_KV, 2 * C_HALF, N_BRANCH * D_MODEL)
IN_COLS = sum(IN_SIZES)

kernel_name = 'hybrid_delta_swa_chunkmlp_step'

F32 = jnp.float32


def _rms_norm(x, g):
    xf = x.astype(F32)
    y = xf * lax.rsqrt(jnp.mean(xf * xf, axis=-1, keepdims=True) + EPS)
    return (y * g.astype(F32)).astype(x.dtype)


def _l2norm(x):
    return x * lax.rsqrt(jnp.sum(x * x, axis=-1, keepdims=True) + EPS)


def _causal_dwconv(x, buf, w):
    width = w.shape[0]
    L = x.shape[1]
    xp = jnp.concatenate([buf.astype(x.dtype), x], axis=1)
    y = xp[:, :L] * w[0]
    for j in range(1, width):
        y = y + xp[:, j:j + L] * w[j]
    return y, xp[:, L:]


def _gated_delta_rule(q, k, v, g, beta, s0):
    n, L, H, dk = q.shape
    dv = v.shape[-1]
    c = math.gcd(L, DELTA_CHUNK)
    nc = L // c

    def chunks(t):
        t = t.reshape((n, nc, c, H) + t.shape[3:])
        return jnp.moveaxis(t, (1, 3), (0, 2))

    qc, kc, vc, bc = chunks(q), chunks(k), chunks(v), chunks(beta)
    gc = jnp.cumsum(chunks(g), axis=-1)
    tri = jnp.tril(jnp.ones((c, c), bool))
    tri_strict = jnp.tril(jnp.ones((c, c), bool), -1)
    diff = gc[..., :, None] - gc[..., None, :]
    decay = jnp.where(tri, jnp.exp(jnp.where(tri, diff, 0.0)), 0.0)
    kk = jnp.einsum('znhrd,znhsd->znhrs', kc, kc)
    a_mat = jnp.where(tri_strict, bc[..., None] * kk * decay, 0.0)
    eye = jnp.eye(c, dtype=F32)
    rhs = jnp.concatenate([bc[..., None] * vc, (bc * jnp.exp(gc))[..., None] * kc], axis=-1)
    sol = lax.linalg.triangular_solve(eye + a_mat, rhs, left_side=True, lower=True, unit_diagonal=True)
    u_base, w_mat = sol[..., :dv], sol[..., dv:]
    qk = jnp.where(tri, jnp.einsum('znhrd,znhsd->znhrs', qc, kc) * decay, 0.0)
    q_dec = qc * jnp.exp(gc)[..., None]
    k_dec = kc * jnp.exp(gc[..., -1:] - gc)[..., None]
    g_last = jnp.exp(gc[..., -1])

    def step(s, xs):
        u_b, w_m, qk_m, q_d, k_d, g_l = xs
        u = u_b - jnp.einsum('nhrk,nhkv->nhrv', w_m, s)
        o = jnp.einsum('nhrk,nhkv->nhrv', q_d, s) + jnp.einsum('nhrs,nhsv->nhrv', qk_m, u)
        s = g_l[..., None, None] * s + jnp.einsum('nhsk,nhsv->nhkv', k_d, u)
        return s, o

    s_final, o = lax.scan(step, s0, (u_base, w_mat, qk, q_dec, k_dec, g_last))
    o = jnp.moveaxis(o, (0, 2), (1, 3)).reshape(n, L, H, dv)
    return o, s_final


def _delta_mixer(qkv, z, b_raw, a_raw, conv_buf, s0, conv_w, a_log, dt_bias, out_g):
    n, L, _ = qkv.shape
    y, conv_buf_new = _causal_dwconv(qkv, conv_buf, conv_w)
    y = jax.nn.silu(y.astype(F32))
    q, k, v = jnp.split(y, [A_QK, 2 * A_QK], axis=-1)
    q = _l2norm(q.reshape(n, L, A_HEADS, A_DK)) * (A_DK ** -0.5)
    k = _l2norm(k.reshape(n, L, A_HEADS, A_DK))
    v = v.reshape(n, L, A_HEADS, A_DV)
    beta = jax.nn.sigmoid(b_raw.astype(F32))
    g = -jnp.exp(a_log.astype(F32)) * jax.nn.softplus(a_raw.astype(F32) + dt_bias.astype(F32))
    o, s_new = _gated_delta_rule(q, k, v, g, beta, s0.astype(F32))
    o = o * lax.rsqrt(jnp.mean(o * o, axis=-1, keepdims=True) + EPS) * out_g.astype(F32)
    o = o * jax.nn.silu(z.astype(F32)).reshape(n, L, A_HEADS, A_DV)
    return o.reshape(n, L, A_VAL).astype(qkv.dtype), conv_buf_new, s_new.astype(s0.dtype)


def _sink_softmax(s, mask, sinks):
    s = jnp.where(mask, s, -jnp.inf)
    sk = sinks.astype(F32).reshape(B_KV_HEADS, B_GROUP, 1, 1)
    m = jnp.maximum(jnp.max(s, axis=-1, keepdims=True), sk)
    e = jnp.exp(s - m)
    return e / (jnp.sum(e, axis=-1, keepdims=True) + jnp.exp(sk - m))


def _window_attn_prompt(q, k, v, sinks):
    n, L = q.shape[:2]
    nb = L // WINDOW
    qb = q.reshape(n, nb, WINDOW, B_KV_HEADS, B_GROUP, B_HD)
    kb = k.reshape(n, nb, WINDOW, B_KV_HEADS, B_HD)
    vb = v.reshape(n, nb, WINDOW, B_KV_HEADS, B_HD)
    pad = ((0, 0), (1, 0), (0, 0), (0, 0), (0, 0))
    kk = jnp.concatenate([jnp.pad(kb, pad)[:, :-1], kb], axis=2)
    vv = jnp.concatenate([jnp.pad(vb, pad)[:, :-1], vb], axis=2)
    rel = (jnp.arange(WINDOW)[:, None] + WINDOW) - jnp.arange(2 * WINDOW)[None, :]
    band = (rel >= 0) & (rel < WINDOW)
    valid = (jnp.arange(nb)[:, None] * WINDOW - WINDOW + jnp.arange(2 * WINDOW)[None, :]) >= 0
    mask = (band[None] & valid[:, None, :])[:, None, None]
    s = jnp.einsum('nbqkgd,nbskd->nbkgqs', qb, kk).astype(F32) * (B_HD ** -0.5)
    pr = _sink_softmax(s, mask, sinks)
    o = jnp.einsum('nbkgqs,nbskd->nbqkgd', pr.astype(vv.dtype), vv)
    return o.reshape(n, L, B_Q)


def _window_attn_sample(q, k, v, k_buf, v_buf, sinks):
    n, L = q.shape[:2]
    wb = k_buf.shape[1]
    kk = jnp.concatenate([k_buf.astype(k.dtype), k], axis=1)
    vv = jnp.concatenate([v_buf.astype(v.dtype), v], axis=1)
    rel = (wb + jnp.arange(L))[:, None] - jnp.arange(wb + L)[None, :]
    mask = (rel >= 0) & (rel < WINDOW)
    s = jnp.einsum('nqkgd,nskd->nkgqs', q, kk).astype(F32) * (B_HD ** -0.5)
    pr = _sink_softmax(s, mask, sinks)
    o = jnp.einsum('nkgqs,nskd->nqkgd', pr.astype(vv.dtype), vv).reshape(n, L, B_Q)
    return o, kk[:, L:], vv[:, L:]


def _chunk_mlp(uv, ln_g, ln_b, w_s, b_s):
    n, L, _ = uv.shape
    u, v = jnp.split(jax.nn.gelu(uv, approximate=True), 2, axis=-1)
    vf = v.astype(F32)
    mu = jnp.mean(vf, axis=-1, keepdims=True)
    var = jnp.mean(jnp.square(vf - mu), axis=-1, keepdims=True)
    vn = ((vf - mu) * lax.rsqrt(var + EPS) * ln_g.astype(F32) + ln_b.astype(F32)).astype(uv.dtype)
    nc = -(-L // C_CHUNK)
    vp = jnp.pad(vn, ((0, 0), (0, nc * C_CHUNK - L), (0, 0))).reshape(n, nc, C_CHUNK, C_GROUPS, C_GC)
    w_causal = jnp.where(jnp.tril(jnp.ones((C_CHUNK, C_CHUNK), bool)), w_s, 0.0)
    mixed = jnp.einsum('gts,ncsgd->nctgd', w_causal, vp) + b_s.T[:, :, None]
    mixed = mixed.reshape(n, nc * C_CHUNK, C_HALF)[:, :L]
    return u * mixed, vn


def _conv_ffn(x, buf, w_up, conv_w, conv_b, w_down):
    hc, buf_new = _causal_dwconv(x @ w_up, buf, conv_w)
    g, up = jnp.split(hc + conv_b, 2, axis=-1)
    return (jax.nn.gelu(g, approximate=True) * up) @ w_down, buf_new


def _layer(h, p_i, prm, i, past):
    n, L, _ = h.shape
    dt = h.dtype
    if past is None:
        a_buf = jnp.zeros((n, A_CONV - 1, A_CONV_CH), dt)
        s0 = jnp.zeros((n, A_HEADS, A_DK, A_DV), dt)
        f_buf = jnp.zeros((n, FFN_CONV - 1, 2 * D_FF), dt)
    else:
        a_buf, s0, k_buf, v_buf, f_buf = past
    xn = _rms_norm(h, prm['norm_mix_pre'][i])
    split_at = np.cumsum(IN_SIZES)[:-1].tolist()
    a_qkv, a_z, a_b, a_a, b_q, b_k, b_v, c_uv, gate_logits = jnp.split(xn @ prm['w_in'][i], split_at, axis=-1)
    y_a, a_buf_new, s_new = _delta_mixer(a_qkv, a_z, a_b, a_a, a_buf, s0, prm['a_conv_w'][i],
                                         prm['a_log'][i], prm['a_dt_bias'][i], prm['a_out_norm'][i])
    q = b_q.reshape(n, L, B_KV_HEADS, B_GROUP, B_HD)
    k = b_k.reshape(n, L, B_KV_HEADS, B_HD)
    v = b_v.reshape(n, L, B_KV_HEADS, B_HD)
    if past is None:
        y_b = _window_attn_prompt(q, k, v, prm['b_sinks'][i])
        wb = min(WINDOW, L)
        k_new, v_new = k[:, L - wb:], v[:, L - wb:]
    else:
        y_b, k_new, v_new = _window_attn_sample(q, k, v, k_buf, v_buf, prm['b_sinks'][i])
    y_c, c_v = _chunk_mlp(c_uv, prm['c_ln_g'][i], prm['c_ln_b'][i], prm['c_w_s'][i], prm['c_b_s'][i])
    g_a, g_b, g_c = jnp.split(jax.nn.sigmoid(gate_logits), N_BRANCH, axis=-1)
    merged = (g_a * (y_a @ prm['w_br_a'][i]) + g_b * (y_b.astype(dt) @ prm['w_br_b'][i])
              + g_c * (y_c @ prm['w_br_c'][i]))
    h = h + _rms_norm(merged @ prm['w_o'][i], prm['norm_mix_post'][i])
    f, f_buf_new = _conv_ffn(_rms_norm(h, prm['norm_ffn_pre'][i]), f_buf, prm['w_up'][i],
                             prm['ffn_conv_w'][i], prm['ffn_conv_b'][i], prm['w_down'][i])
    h = h + _rms_norm(f, prm['norm_ffn_post'][i])
    gate = jax.nn.sigmoid(_rms_norm(h, prm['norm_ple'][i]) @ prm['w_ple_gate'][i])
    h = h + gate * (p_i @ prm['w_ple'][i])
    return h, (a_buf_new, s_new, k_new, v_new, f_buf_new, c_v)


def _run_group(x, p, prm, past):
    h = x
    states = []
    for i in range(DEPTH):
        past_i = None if past is None else [a[i] for a in past]
        h, st = _layer(h, p[i], prm, i, past_i)
        states.append(st)
    return h, states


def _stack(states, j):
    return jnp.stack([s[j] for s in states])


def setup_inputs(seed: int = 0) -> dict:
    key = jax.random.key(seed)
    keys = list(jax.random.split(key, 40))

    def nrm(shape, scale=1.0):
        return jax.random.normal(keys.pop(), shape, F32) * scale

    def gain(shape):
        return 1.0 + nrm(shape, 0.02)

    wb = min(WINDOW, PAST_LEN)
    dt0 = jnp.exp(jax.random.uniform(keys.pop(), (DEPTH, A_HEADS), F32)
                  * (math.log(0.1) - math.log(1e-3)) + math.log(1e-3))
    return {
        'x_prompt': nrm((BATCH, SEQ, D_MODEL)),
        'x_sample': nrm((DEC_BATCH, DEC_SEQ, D_MODEL)),
        'p_prompt': nrm((DEPTH, BATCH, SEQ, PLE_DIM)),
        'p_sample': nrm((DEPTH, DEC_BATCH, DEC_SEQ, PLE_DIM)),
        'state_a_conv': nrm((DEPTH, DEC_BATCH, A_CONV - 1, A_CONV_CH)),
        'state_delta': nrm((DEPTH, DEC_BATCH, A_HEADS, A_DK, A_DV), 0.3),
        'cache_win_k': nrm((DEPTH, DEC_BATCH, wb, B_KV_HEADS, B_HD)),
        'cache_win_v': nrm((DEPTH, DEC_BATCH, wb, B_KV_HEADS, B_HD)),
        'state_ffn_conv': nrm((DEPTH, DEC_BATCH, FFN_CONV - 1, 2 * D_FF)),
        'norm_mix_pre': gain((DEPTH, D_MODEL)),
        'norm_mix_post': gain((DEPTH, D_MODEL)),
        'norm_ffn_pre': gain((DEPTH, D_MODEL)),
        'norm_ffn_post': gain((DEPTH, D_MODEL)),
        'norm_ple': gain((DEPTH, D_MODEL)),
        'w_in': nrm((DEPTH, D_MODEL, IN_COLS), D_MODEL ** -0.5),
        'a_conv_w': nrm((DEPTH, A_CONV, A_CONV_CH), A_CONV ** -0.5),
        'a_log': jnp.log(jax.random.uniform(keys.pop(), (DEPTH, A_HEADS), F32, 1.0, 16.0)),
        'a_dt_bias': dt0 + jnp.log(-jnp.expm1(-dt0)),
        'a_out_norm': gain((DEPTH, A_DV)),
        'b_sinks': nrm((DEPTH, B_HEADS)),
        'c_ln_g': gain((DEPTH, C_HALF)),
        'c_ln_b': nrm((DEPTH, C_HALF), 0.02),
        'c_w_s': nrm((DEPTH, C_GROUPS, C_CHUNK, C_CHUNK), C_CHUNK ** -0.5),
        'c_b_s': gain((DEPTH, C_GROUPS, C_CHUNK)),
        'w_br_a': nrm((DEPTH, A_VAL, D_MODEL), A_VAL ** -0.5),
        'w_br_b': nrm((DEPTH, B_Q, D_MODEL), B_Q ** -0.5),
        'w_br_c': nrm((DEPTH, C_HALF, D_MODEL), C_HALF ** -0.5),
        'w_o': nrm((DEPTH, D_MODEL, D_MODEL), D_MODEL ** -0.5),
        'w_up': nrm((DEPTH, D_MODEL, 2 * D_FF), D_MODEL ** -0.5),
        'ffn_conv_w': nrm((DEPTH, FFN_CONV, 2 * D_FF), FFN_CONV ** -0.5),
        'ffn_conv_b': nrm((DEPTH, 2 * D_FF), 0.02),
        'w_down': nrm((DEPTH, D_FF, D_MODEL), D_FF ** -0.5),
        'w_ple': nrm((DEPTH, PLE_DIM, D_MODEL), PLE_DIM ** -0.5),
        'w_ple_gate': nrm((DEPTH, D_MODEL, D_MODEL), D_MODEL ** -0.5),
    }


def reference(x_prompt, x_sample, p_prompt, p_sample, state_a_conv, state_delta, cache_win_k, cache_win_v,
              state_ffn_conv, norm_mix_pre, norm_mix_post, norm_ffn_pre, norm_ffn_post, norm_ple, w_in,
              a_conv_w, a_log, a_dt_bias, a_out_norm, b_sinks, c_ln_g, c_ln_b, c_w_s, c_b_s,
              w_br_a, w_br_b, w_br_c, w_o, w_up, ffn_conv_w, ffn_conv_b, w_down, w_ple, w_ple_gate):
    prm = {
        'norm_mix_pre': norm_mix_pre, 'norm_mix_post': norm_mix_post,
        'norm_ffn_pre': norm_ffn_pre, 'norm_ffn_post': norm_ffn_post, 'norm_ple': norm_ple,
        'w_in': w_in, 'a_conv_w': a_conv_w, 'a_log': a_log, 'a_dt_bias': a_dt_bias,
        'a_out_norm': a_out_norm, 'b_sinks': b_sinks, 'c_ln_g': c_ln_g, 'c_ln_b': c_ln_b,
        'c_w_s': c_w_s, 'c_b_s': c_b_s, 'w_br_a': w_br_a, 'w_br_b': w_br_b, 'w_br_c': w_br_c,
        'w_o': w_o, 'w_up': w_up, 'ffn_conv_w': ffn_conv_w, 'ffn_conv_b': ffn_conv_b,
        'w_down': w_down, 'w_ple': w_ple, 'w_ple_gate': w_ple_gate,
    }
    y_prompt, st_p = _run_group(x_prompt, p_prompt, prm, None)
    y_sample, st_s = _run_group(x_sample, p_sample, prm,
                                (state_a_conv, state_delta, cache_win_k, cache_win_v, state_ffn_conv))
    return (y_prompt, y_sample,
            _stack(st_p, 0), _stack(st_p, 1), _stack(st_p, 2), _stack(st_p, 3), _stack(st_p, 4),
            _stack(st_s, 0), _stack(st_s, 1), _stack(st_s, 2), _stack(st_s, 3), _stack(st_s, 4),
            _stack(st_s, 5))
```

```python
import functools
import math

import jax
import jax.numpy as jnp
import numpy as np
from jax import lax
from jax.experimental import pallas as pl
from jax.experimental.pallas import tpu as pltpu

D_MODEL = 4096
DEPTH = 4
A_HEADS = 8
A_DK = 128
A_DV = 128
A_QK = A_HEADS * A_DK
A_VAL = A_HEADS * A_DV
A_CONV = 4
A_CONV_CH = 2 * A_QK + A_VAL
DELTA_CHUNK = 64
B_HEADS = 16
B_KV_HEADS = 4
B_GROUP = B_HEADS // B_KV_HEADS
B_HD = 64
B_Q = B_HEADS * B_HD
B_KV = B_KV_HEADS * B_HD
WINDOW = 128
C_GROUPS = 8
C_GC = 128
C_HALF = C_GROUPS * C_GC
C_CHUNK = 128
D_FF = 3 * D_MODEL
FFN_CONV = 3
N_BRANCH = 3
EPS = 1e-6

F32 = jnp.float32
BF16 = jnp.bfloat16

VMEM_LIMIT_BYTES = 56 * 1024 * 1024


def _mm_kernel(x_ref, w_ref, o_ref, wbf_ref):
    @pl.when(pl.program_id(1) == 0)
    def _():
        wbf_ref[...] = w_ref[...].astype(BF16)

    o_ref[...] = jnp.dot(x_ref[...], wbf_ref[...], preferred_element_type=F32)


def _mm_acc_kernel(x_ref, w_ref, a_ref, o_ref, wbf_ref):
    @pl.when(pl.program_id(1) == 0)
    def _():
        wbf_ref[...] = w_ref[...].astype(BF16)

    o_ref[...] = a_ref[...] + jnp.dot(x_ref[...], wbf_ref[...], preferred_element_type=F32)


def _matmul(x, w, *, bm, bn, n_cols=None, col0=0, k_chunk=0, bk=None, acc=None):
    m = x.shape[0]
    bk = w.shape[0] if bk is None else bk
    n_cols = w.shape[1] if n_cols is None else n_cols
    assert m % bm == 0 and n_cols % bn == 0 and col0 % bn == 0
    cb0 = col0 // bn
    grid = (n_cols // bn, m // bm)
    x_spec = pl.BlockSpec((bm, bk), lambda j, i: (i, k_chunk))
    w_spec = pl.BlockSpec((bk, bn), lambda j, i: (k_chunk, cb0 + j))
    o_spec = pl.BlockSpec((bm, bn), lambda j, i: (i, j))
    params = pltpu.CompilerParams(dimension_semantics=("arbitrary", "arbitrary"),
                                  vmem_limit_bytes=VMEM_LIMIT_BYTES)
    out_shape = jax.ShapeDtypeStruct((m, n_cols), F32)
    scratch = [pltpu.VMEM((bk, bn), BF16)]
    if acc is None:
        return pl.pallas_call(_mm_kernel, grid=grid, in_specs=[x_spec, w_spec], out_specs=o_spec,
                              out_shape=out_shape, scratch_shapes=scratch, compiler_params=params)(x, w)
    return pl.pallas_call(_mm_acc_kernel, grid=grid, in_specs=[x_spec, w_spec, o_spec], out_specs=o_spec,
                          out_shape=out_shape, scratch_shapes=scratch, compiler_params=params,
                          input_output_aliases={2: 0})(x, w, acc)


def _rms_norm(x, g):
    xf = x.astype(F32)
    y = xf * lax.rsqrt(jnp.mean(xf * xf, axis=-1, keepdims=True) + EPS)
    return y * g.astype(F32)


def _l2norm(x):
    return x * lax.rsqrt(jnp.sum(x * x, axis=-1, keepdims=True) + EPS)


def _causal_dwconv(x, buf, w):
    width = w.shape[0]
    L = x.shape[1]
    xp = jnp.concatenate([buf.astype(x.dtype), x], axis=1)
    y = xp[:, :L] * w[0]
    for j in range(1, width):
        y = y + xp[:, j:j + L] * w[j]
    return y, xp[:, L:]


def _gated_delta_rule(q, k, v, g, beta, s0):
    n, L, H, dk = q.shape
    dv = v.shape[-1]
    c = math.gcd(L, DELTA_CHUNK)
    nc = L // c

    def chunks(t):
        t = t.reshape((n, nc, c, H) + t.shape[3:])
        return jnp.moveaxis(t, (1, 3), (0, 2))

    qc, kc, vc, bc = chunks(q), chunks(k), chunks(v), chunks(beta)
    gc = jnp.cumsum(chunks(g), axis=-1)
    tri = jnp.tril(jnp.ones((c, c), bool))
    tri_strict = jnp.tril(jnp.ones((c, c), bool), -1)
    diff = gc[..., :, None] - gc[..., None, :]
    decay = jnp.where(tri, jnp.exp(jnp.where(tri, diff, 0.0)), 0.0)
    kk = jnp.einsum('znhrd,znhsd->znhrs', kc, kc)
    a_mat = jnp.where(tri_strict, bc[..., None] * kk * decay, 0.0)
    eye = jnp.eye(c, dtype=F32)
    rhs = jnp.concatenate([bc[..., None] * vc, (bc * jnp.exp(gc))[..., None] * kc], axis=-1)
    sol = lax.linalg.triangular_solve(eye + a_mat, rhs, left_side=True, lower=True, unit_diagonal=True)
    u_base, w_mat = sol[..., :dv], sol[..., dv:]
    qk = jnp.where(tri, jnp.einsum('znhrd,znhsd->znhrs', qc, kc) * decay, 0.0)
    q_dec = qc * jnp.exp(gc)[..., None]
    k_dec = kc * jnp.exp(gc[..., -1:] - gc)[..., None]
    g_last = jnp.exp(gc[..., -1])

    def step(s, xs):
        u_b, w_m, qk_m, q_d, k_d, g_l = xs
        u = u_b - jnp.einsum('nhrk,nhkv->nhrv', w_m, s)
        o = jnp.einsum('nhrk,nhkv->nhrv', q_d, s) + jnp.einsum('nhrs,nhsv->nhrv', qk_m, u)
        s = g_l[..., None, None] * s + jnp.einsum('nhsk,nhsv->nhkv', k_d, u)
        return s, o

    s_final, o = lax.scan(step, s0, (u_base, w_mat, qk, q_dec, k_dec, g_last))
    o = jnp.moveaxis(o, (0, 2), (1, 3)).reshape(n, L, H, dv)
    return o, s_final


def _delta_mixer(qkv, z, b_raw, a_raw, conv_buf, s0, conv_w, a_log, dt_bias, out_g):
    n, L, _ = qkv.shape
    y, conv_buf_new = _causal_dwconv(qkv, conv_buf, conv_w)
    y = jax.nn.silu(y.astype(F32))
    q, k, v = jnp.split(y, [A_QK, 2 * A_QK], axis=-1)
    q = _l2norm(q.reshape(n, L, A_HEADS, A_DK)) * (A_DK ** -0.5)
    k = _l2norm(k.reshape(n, L, A_HEADS, A_DK))
    v = v.reshape(n, L, A_HEADS, A_DV)
    beta = jax.nn.sigmoid(b_raw.astype(F32))
    g = -jnp.exp(a_log.astype(F32)) * jax.nn.softplus(a_raw.astype(F32) + dt_bias.astype(F32))
    o, s_new = _gated_delta_rule(q, k, v, g, beta, s0.astype(F32))
    o = o * lax.rsqrt(jnp.mean(o * o, axis=-1, keepdims=True) + EPS) * out_g.astype(F32)
    o = o * jax.nn.silu(z.astype(F32)).reshape(n, L, A_HEADS, A_DV)
    return o.reshape(n, L, A_VAL), conv_buf_new, s_new


def _sink_softmax(s, mask, sinks):
    s = jnp.where(mask, s, -jnp.inf)
    sk = sinks.astype(F32).reshape(B_KV_HEADS, B_GROUP, 1, 1)
    m = jnp.maximum(jnp.max(s, axis=-1, keepdims=True), sk)
    e = jnp.exp(s - m)
    return e / (jnp.sum(e, axis=-1, keepdims=True) + jnp.exp(sk - m))


def _window_attn_prompt(q, k, v, sinks):
    n, L = q.shape[:2]
    nb = L // WINDOW
    qb = q.reshape(n, nb, WINDOW, B_KV_HEADS, B_GROUP, B_HD)
    kb = k.reshape(n, nb, WINDOW, B_KV_HEADS, B_HD)
    vb = v.reshape(n, nb, WINDOW, B_KV_HEADS, B_HD)
    pad = ((0, 0), (1, 0), (0, 0), (0, 0), (0, 0))
    kk = jnp.concatenate([jnp.pad(kb, pad)[:, :-1], kb], axis=2)
    vv = jnp.concatenate([jnp.pad(vb, pad)[:, :-1], vb], axis=2)
    rel = (jnp.arange(WINDOW)[:, None] + WINDOW) - jnp.arange(2 * WINDOW)[None, :]
    band = (rel >= 0) & (rel < WINDOW)
    valid = (jnp.arange(nb)[:, None] * WINDOW - WINDOW + jnp.arange(2 * WINDOW)[None, :]) >= 0
    mask = (band[None] & valid[:, None, :])[:, None, None]
    s = jnp.einsum('nbqkgd,nbskd->nbkgqs', qb, kk).astype(F32) * (B_HD ** -0.5)
    pr = _sink_softmax(s, mask, sinks)
    o = jnp.einsum('nbkgqs,nbskd->nbqkgd', pr.astype(vv.dtype), vv)
    return o.reshape(n, L, B_Q)


def _window_attn_sample(q, k, v, k_buf, v_buf, sinks):
    n, L = q.shape[:2]
    wb = k_buf.shape[1]
    kk = jnp.concatenate([k_buf.astype(k.dtype), k], axis=1)
    vv = jnp.concatenate([v_buf.astype(v.dtype), v], axis=1)
    rel = (wb + jnp.arange(L))[:, None] - jnp.arange(wb + L)[None, :]
    mask = (rel >= 0) & (rel < WINDOW)
    s = jnp.einsum('nqkgd,nskd->nkgqs', q, kk).astype(F32) * (B_HD ** -0.5)
    pr = _sink_softmax(s, mask, sinks)
    o = jnp.einsum('nkgqs,nskd->nqkgd', pr.astype(vv.dtype), vv).reshape(n, L, B_Q)
    return o, kk[:, L:], vv[:, L:]


def _chunk_mlp(uv, ln_g, ln_b, w_s, b_s):
    n, L, _ = uv.shape
    u, v = jnp.split(jax.nn.gelu(uv, approximate=True), 2, axis=-1)
    vf = v.astype(F32)
    mu = jnp.mean(vf, axis=-1, keepdims=True)
    var = jnp.mean(jnp.square(vf - mu), axis=-1, keepdims=True)
    vn = (vf - mu) * lax.rsqrt(var + EPS) * ln_g.astype(F32) + ln_b.astype(F32)
    nc = -(-L // C_CHUNK)
    vp = jnp.pad(vn, ((0, 0), (0, nc * C_CHUNK - L), (0, 0))).reshape(n, nc, C_CHUNK, C_GROUPS, C_GC)
    w_causal = jnp.where(jnp.tril(jnp.ones((C_CHUNK, C_CHUNK), bool)), w_s, 0.0)
    mixed = jnp.einsum('gts,ncsgd->nctgd', w_causal, vp) + b_s.T[:, :, None]
    mixed = mixed.reshape(n, nc * C_CHUNK, C_HALF)[:, :L]
    return u * mixed, vn


BM = 1056
BN = 512


def _mixers(proj_a, proj_ba, proj_r, n, L, prm, i, past):
    a_qkv = proj_a[:, :A_CONV_CH].reshape(n, L, A_CONV_CH)
    a_z = proj_a[:, A_CONV_CH:].reshape(n, L, A_VAL)
    a_b = proj_ba[:, :A_HEADS].reshape(n, L, A_HEADS)
    a_a = proj_ba[:, A_HEADS:2 * A_HEADS].reshape(n, L, A_HEADS)
    o = 0
    b_q = proj_r[:, o:o + B_Q]; o += B_Q
    b_k = proj_r[:, o:o + B_KV]; o += B_KV
    b_v = proj_r[:, o:o + B_KV]; o += B_KV
    c_uv = proj_r[:, o:o + 2 * C_HALF].reshape(n, L, 2 * C_HALF); o += 2 * C_HALF
    if past is None:
        a_buf = jnp.zeros((n, A_CONV - 1, A_CONV_CH), F32)
        s0 = jnp.zeros((n, A_HEADS, A_DK, A_DV), F32)
    else:
        a_buf, s0, k_buf, v_buf = past
    y_a, a_buf_new, s_new = _delta_mixer(a_qkv, a_z, a_b, a_a, a_buf, s0, prm['a_conv_w'][i],
                                         prm['a_log'][i], prm['a_dt_bias'][i], prm['a_out_norm'][i])
    q = b_q.reshape(n, L, B_KV_HEADS, B_GROUP, B_HD)
    k = b_k.reshape(n, L, B_KV_HEADS, B_HD)
    v = b_v.reshape(n, L, B_KV_HEADS, B_HD)
    if past is None:
        y_b = _window_attn_prompt(q, k, v, prm['b_sinks'][i])
        wb = min(WINDOW, L)
        k_new, v_new = k[:, L - wb:], v[:, L - wb:]
    else:
        y_b, k_new, v_new = _window_attn_sample(q, k, v, k_buf, v_buf, prm['b_sinks'][i])
    y_c, c_v = _chunk_mlp(c_uv, prm['c_ln_g'][i], prm['c_ln_b'][i], prm['c_w_s'][i], prm['c_b_s'][i])
    t = n * L
    return (y_a.reshape(t, A_VAL), y_b.reshape(t, B_Q), y_c.reshape(t, C_HALF),
            (a_buf_new, s_new, k_new, v_new, c_v))


def kernel(x_prompt, x_sample, p_prompt, p_sample, state_a_conv, state_delta, cache_win_k, cache_win_v, state_ffn_conv, norm_mix_pre, norm_mix_post, norm_ffn_pre, norm_ffn_post, norm_ple, w_in, a_conv_w, a_log, a_dt_bias, a_out_norm, b_sinks, c_ln_g, c_ln_b, c_w_s, c_b_s, w_br_a, w_br_b, w_br_c, w_o, w_up, ffn_conv_w, ffn_conv_b, w_down, w_ple, w_ple_gate):
    prm = {
        'a_conv_w': a_conv_w, 'a_log': a_log, 'a_dt_bias': a_dt_bias, 'a_out_norm': a_out_norm,
        'b_sinks': b_sinks, 'c_ln_g': c_ln_g, 'c_ln_b': c_ln_b, 'c_w_s': c_w_s, 'c_b_s': c_b_s,
    }
    nb, ls = x_prompt.shape[:2]
    ns, lq = x_sample.shape[:2]
    tp, ts = nb * ls, ns * lq
    t = tp + ts
    ple = p_prompt.shape[-1]
    h = jnp.concatenate([x_prompt.reshape(tp, D_MODEL), x_sample.reshape(ts, D_MODEL)], axis=0)
    off_ba = A_CONV_CH + A_VAL
    off_rest = off_ba + 2 * A_HEADS
    n_rest = w_in.shape[-1] - off_rest
    st_p, st_s = [], []
    for i in range(DEPTH):
        xn = _rms_norm(h, norm_mix_pre[i]).astype(BF16)
        w_ba = jnp.pad(w_in[i][:, off_ba:off_rest], ((0, 0), (0, 128 - 2 * A_HEADS)))
        w_rest = w_in[i][:, off_rest:]
        proj_a = _matmul(xn, w_in[i], bm=BM, bn=BN, n_cols=off_ba)
        proj_ba = _matmul(xn, w_ba, bm=BM, bn=128)
        proj_r = _matmul(xn, w_rest, bm=BM, bn=BN)
        ya_p, yb_p, yc_p, sp = _mixers(proj_a[:tp], proj_ba[:tp], proj_r[:tp], nb, ls, prm, i, None)
        past = (state_a_conv[i], state_delta[i], cache_win_k[i], cache_win_v[i])
        ya_s, yb_s, yc_s, ss = _mixers(proj_a[tp:], proj_ba[tp:], proj_r[tp:], ns, lq, prm, i, past)
        y_a = jnp.concatenate([ya_p, ya_s]).astype(BF16)
        y_b = jnp.concatenate([yb_p, yb_s]).astype(BF16)
        y_c = jnp.concatenate([yc_p, yc_s]).astype(BF16)
        gates = jax.nn.sigmoid(proj_r[:, n_rest - N_BRANCH * D_MODEL:])
        g_a, g_b, g_c = jnp.split(gates, N_BRANCH, axis=-1)
        merged = (g_a * _matmul(y_a, w_br_a[i], bm=BM, bn=BN) + g_b * _matmul(y_b, w_br_b[i], bm=BM, bn=BN)
                  + g_c * _matmul(y_c, w_br_c[i], bm=BM, bn=BN))
        h = h + _rms_norm(_matmul(merged.astype(BF16), w_o[i], bm=BM, bn=BN), norm_mix_post[i])
        xf = _rms_norm(h, norm_ffn_pre[i]).astype(BF16)
        up = _matmul(xf, w_up[i], bm=BM, bn=BN)
        hc_p, fb_p = _causal_dwconv(up[:tp].reshape(nb, ls, 2 * D_FF),
                                    jnp.zeros((nb, FFN_CONV - 1, 2 * D_FF), F32), ffn_conv_w[i])
        hc_s, fb_s = _causal_dwconv(up[tp:].reshape(ns, lq, 2 * D_FF), state_ffn_conv[i], ffn_conv_w[i])
        hc = jnp.concatenate([hc_p.reshape(tp, 2 * D_FF), hc_s.reshape(ts, 2 * D_FF)]) + ffn_conv_b[i]
        gg, uu = jnp.split(hc, 2, axis=-1)
        act = (jax.nn.gelu(gg, approximate=True) * uu).astype(BF16)
        f = None
        for c in range(D_FF // D_MODEL):
            f = _matmul(act, w_down[i], bm=BM, bn=BN, k_chunk=c, bk=D_MODEL, acc=f)
        h = h + _rms_norm(f, norm_ffn_post[i])
        hn = _rms_norm(h, norm_ple[i]).astype(BF16)
        gate = jax.nn.sigmoid(_matmul(hn, w_ple_gate[i], bm=BM, bn=BN))
        p_i = jnp.concatenate([p_prompt[i].reshape(tp, ple), p_sample[i].reshape(ts, ple)]).astype(BF16)
        h = h + gate * _matmul(p_i, w_ple[i], bm=BM, bn=BN)
        st_p.append(sp + (fb_p,))
        st_s.append(ss + (fb_s,))

    def stack(states, j):
        return jnp.stack([s[j] for s in states])

    y_prompt = h[:tp].reshape(nb, ls, D_MODEL)
    y_sample = h[tp:].reshape(ns, lq, D_MODEL)
    return (y_prompt, y_sample,
            stack(st_p, 0), stack(st_p, 1), stack(st_p, 2), stack(st_p, 3), stack(st_p, 5),
            stack(st_s, 0), stack(st_s, 1), stack(st_s, 2), stack(st_s, 3), stack(st_s, 5),
            stack(st_s, 4))
```

```python
import functools
import math

import jax
import jax.numpy as jnp
from jax import lax
from jax.experimental import pallas as pl
from jax.experimental.pallas import tpu as pltpu

D_MODEL = 4096
DEPTH = 4
A_HEADS = 8
A_DK = 128
A_DV = 128
A_QK = A_HEADS * A_DK
A_VAL = A_HEADS * A_DV
A_CONV = 4
A_CONV_CH = 2 * A_QK + A_VAL
DELTA_CHUNK = 64
B_HEADS = 16
B_KV_HEADS = 4
B_GROUP = B_HEADS // B_KV_HEADS
B_HD = 64
B_Q = B_HEADS * B_HD
B_KV = B_KV_HEADS * B_HD
WINDOW = 128
C_GROUPS = 8
C_GC = 128
C_HALF = C_GROUPS * C_GC
C_CHUNK = 128
D_FF = 3 * D_MODEL
FFN_CONV = 3
N_BRANCH = 3
EPS = 1e-6

F32 = jnp.float32
BF16 = jnp.bfloat16
LANES = 128

VMEM_LIMIT_BYTES = 56 * 1024 * 1024

DELTA_RT = 256
DELTA_HP = 2
NEUMANN_BLOCK = 16


def _sigmoid(x):
    return 1.0 / (1.0 + jnp.exp(-x))


def _softplus(x):
    return jnp.maximum(x, 0.0) + jnp.log(1.0 + jnp.exp(-jnp.abs(x)))


def _gelu_tanh(x):
    return 0.5 * x * (1.0 + jnp.tanh(0.7978845608028654 * (x + 0.044715 * (x * x * x))))


def _dot(a, b):
    return jnp.dot(a, b, preferred_element_type=F32)


def _dot_nt(a, b):
    return lax.dot_general(a, b, (((1,), (1,)), ((), ())), preferred_element_type=F32)


def _dot_tn(a, b):
    return lax.dot_general(a, b, (((0,), (0,)), ((), ())), preferred_element_type=F32)


def _split2(x):
    hi = x.astype(BF16)
    lo = (x - hi.astype(F32)).astype(BF16)
    return hi, lo


def _dot3s(a, b):
    ah, al = a
    bh, bl = b
    return _dot(ah, bh) + (_dot(ah, bl) + _dot(al, bh))


def _mm_kernel(x_ref, w_ref, o_ref, wbf_ref):
    @pl.when(pl.program_id(1) == 0)
    def _():
        wbf_ref[...] = w_ref[...].astype(BF16)

    o_ref[...] = _dot(x_ref[...], wbf_ref[...])


def _mm_acc_kernel(x_ref, w_ref, a_ref, o_ref, wbf_ref):
    @pl.when(pl.program_id(1) == 0)
    def _():
        wbf_ref[...] = w_ref[...].astype(BF16)

    o_ref[...] = a_ref[...] + _dot(x_ref[...], wbf_ref[...])


def _matmul(x, w, *, bm, bn, n_cols=None, col0=0, k_chunk=0, bk=None, acc=None):
    m = x.shape[0]
    bk = w.shape[0] if bk is None else bk
    n_cols = w.shape[1] if n_cols is None else n_cols
    assert m % bm == 0 and n_cols % bn == 0 and col0 % bn == 0
    cb0 = col0 // bn
    grid = (n_cols // bn, m // bm)
    x_spec = pl.BlockSpec((bm, bk), lambda j, i: (i, k_chunk))
    w_spec = pl.BlockSpec((bk, bn), lambda j, i: (k_chunk, cb0 + j))
    o_spec = pl.BlockSpec((bm, bn), lambda j, i: (i, j))
    params = pltpu.CompilerParams(dimension_semantics=("arbitrary", "arbitrary"),
                                  vmem_limit_bytes=VMEM_LIMIT_BYTES)
    out_shape = jax.ShapeDtypeStruct((m, n_cols), F32)
    scratch = [pltpu.VMEM((bk, bn), BF16)]
    if acc is None:
        return pl.pallas_call(_mm_kernel, grid=grid, in_specs=[x_spec, w_spec], out_specs=o_spec,
                              out_shape=out_shape, scratch_shapes=scratch, compiler_params=params,
                              name="proj")(x, w)
    return pl.pallas_call(_mm_acc_kernel, grid=grid, in_specs=[x_spec, w_spec, o_spec], out_specs=o_spec,
                          out_shape=out_shape, scratch_shapes=scratch, compiler_params=params,
                          input_output_aliases={2: 0}, name="proj_acc")(x, w, acc)


def _unit_lower_inverse(a, rx, c):
    eye = (rx == 0).astype(F32)
    blk = min(NEUMANN_BLOCK, c)
    n = -jnp.where(rx < blk, a, 0.0)
    p = eye + n
    pw = n
    for _ in range(int(math.log2(blk)) - 1):
        pws = _split2(pw)
        pw = _dot3s(pws, pws)
        p = p + _dot3s(_split2(p), _split2(pw))
    while blk < c:
        e = jnp.where((rx >= blk) & (rx < 2 * blk), a, 0.0)
        ps = _split2(p)
        p = p - _dot3s(ps, _split2(_dot3s(_split2(e), ps)))
        blk *= 2
    return p


def _conv_silu(x, xprev, cw, seg):
    pos = lax.broadcasted_iota(jnp.int32, x.shape, 0) & (seg - 1)
    acc = x * cw[3:4, :]
    for j in range(1, 4):
        sh = jnp.where(pos < j, pltpu.roll(xprev, j, axis=0), pltpu.roll(x, j, axis=0))
        acc = acc + sh * cw[3 - j:4 - j, :]
    return acc * _sigmoid(acc)


def _delta_tile_prep(q_raw, k_raw, v_raw, pq, pk, pv, cwq, cwk, cwv, beta_col, g_col, c, seg):
    rt = q_raw.shape[0]
    q = _conv_silu(q_raw, pq, cwq, seg)
    k = _conv_silu(k_raw, pk, cwk, seg)
    v = _conv_silu(v_raw, pv, cwv, seg)
    q = q * (lax.rsqrt(jnp.sum(q * q, axis=-1, keepdims=True) + EPS) * (A_DK ** -0.5))
    k = k * lax.rsqrt(jnp.sum(k * k, axis=-1, keepdims=True) + EPS)

    r_i = lax.broadcasted_iota(jnp.int32, (rt, rt), 0)
    c_i = lax.broadcasted_iota(jnp.int32, (rt, rt), 1)
    rx = r_i ^ c_i
    same = rx < c
    lower = same & (c_i <= r_i)
    strict = same & (c_i < r_i)

    g1 = g_col.astype(BF16)
    r1 = g_col - g1.astype(F32)
    g2 = r1.astype(BF16)
    g3 = (r1 - g2.astype(F32)).astype(BF16)
    lmat = lower.astype(BF16)
    umat = (same & (r_i <= c_i)).astype(BF16)
    ones = jnp.ones((rt, rt), BF16)
    gc = None
    gr = None
    for gp in (g1, g2, g3):
        a = _dot(lmat, jnp.broadcast_to(gp, (rt, LANES)))
        b = _dot(ones, jnp.broadcast_to(gp, (rt, rt)) * umat)
        gc = a if gc is None else gc + a
        gr = b if gr is None else gr + b
    gc_full = jnp.concatenate([gc] * (rt // LANES), axis=1)
    decay = jnp.where(lower, jnp.exp(jnp.where(lower, gc_full - gr, 0.0)), 0.0)

    kb = k.astype(BF16)
    kk = _dot_nt(kb, kb)
    a_mat = jnp.where(strict, beta_col * kk * decay, 0.0)
    t_inv = _unit_lower_inverse(a_mat, rx, c)
    rhs = jnp.concatenate([beta_col * v, (beta_col * jnp.exp(gc)) * k], axis=1)
    sol = _dot3s(_split2(t_inv), _split2(rhs))
    u_base = sol[:, :LANES]
    w_mat = sol[:, LANES:]
    qk = jnp.where(lower, _dot_nt(q.astype(BF16), kb) * decay, 0.0)
    q_dec = q * jnp.exp(gc)
    return k, gc, u_base, w_mat, qk, q_dec


def _gated_out_norm(o, z, og):
    o = o * lax.rsqrt(jnp.mean(o * o, axis=-1, keepdims=True) + EPS) * og
    return o * (z * _sigmoid(z))


def _head_cols(ba, alog_row, dtb_row, h, n_heads):
    lane = lax.broadcasted_iota(jnp.int32, ba.shape, 1)
    sig = _sigmoid(ba)
    g_all = -jnp.exp(alog_row) * _softplus(ba + dtb_row)
    beta_col = jnp.sum(jnp.where(lane == h, sig, 0.0), axis=1, keepdims=True)
    g_col = jnp.sum(jnp.where(lane == n_heads + h, g_all, 0.0), axis=1, keepdims=True)
    return beta_col, g_col


def _delta_prompt_kernel(q_ref, k_ref, v_ref, z_ref, ba_ref, cwq_ref, cwk_ref, cwv_ref, alog_ref, dtb_ref, og_ref,
                         y_ref, sfin_ref, xprev_ref, s_ref, *, n_heads, chunk):
    j = pl.program_id(2)
    hp = pl.program_id(1)
    rt = q_ref.shape[0]

    @pl.when(j == 0)
    def _():
        xprev_ref[...] = jnp.zeros_like(xprev_ref)
        s_ref[...] = jnp.zeros_like(s_ref)

    ba = ba_ref[...]
    for hh in range(DELTA_HP):
        cs = slice(hh * LANES, (hh + 1) * LANES)
        beta_col, g_col = _head_cols(ba, alog_ref[...], dtb_ref[...], hp * DELTA_HP + hh, n_heads)
        q_raw, k_raw, v_raw = q_ref[:, cs], k_ref[:, cs], v_ref[:, cs]
        k, gc, u_base, w_mat, qk, q_dec = _delta_tile_prep(
            q_raw, k_raw, v_raw, xprev_ref[0, :, cs], xprev_ref[1, :, cs], xprev_ref[2, :, cs],
            cwq_ref[:, cs], cwk_ref[:, cs], cwv_ref[:, cs], beta_col, g_col, chunk, rt)
        xprev_ref[0, :, cs] = q_raw
        xprev_ref[1, :, cs] = k_raw
        xprev_ref[2, :, cs] = v_raw
        s = s_ref[hh]
        us, os_ = [], []
        for cc in range(rt // chunk):
            rs = slice(cc * chunk, (cc + 1) * chunk)
            sb = s.astype(BF16)
            u_c = u_base[rs] - _dot(w_mat[rs].astype(BF16), sb)
            os_.append(_dot(q_dec[rs].astype(BF16), sb))
            gl = gc[(cc + 1) * chunk - 1:(cc + 1) * chunk, :]
            k_dec = k[rs] * jnp.exp(gl - gc[rs])
            s = jnp.exp(gl) * s + _dot_tn(k_dec.astype(BF16), u_c.astype(BF16))
            us.append(u_c)
        s_ref[hh] = s
        u_all = jnp.concatenate(us, axis=0)
        o = jnp.concatenate(os_, axis=0) + _dot(qk.astype(BF16), u_all.astype(BF16))
        y_ref[:, cs] = _gated_out_norm(o, z_ref[:, cs], og_ref[...]).astype(y_ref.dtype)

    @pl.when(j == pl.num_programs(2) - 1)
    def _():
        sfin_ref[0] = s_ref[...]


def _delta_prompt(proj_a, proj_ba, conv_w, alog_row, dtb_row, out_g, *, n_seq, seq_len, n_heads, chunk):
    rt, hp = DELTA_RT, DELTA_HP
    assert seq_len % rt == 0 and rt % chunk == 0 and n_heads % hp == 0
    tiles = seq_len // rt
    nhp = n_heads // hp
    wblk = hp * LANES
    grid = (n_seq, nhp, tiles)

    def col_spec(base):
        return pl.BlockSpec((rt, wblk), lambda n, h, j: (n * tiles + j, base + h))

    def cw_spec(base):
        return pl.BlockSpec((4, wblk), lambda n, h, j: (0, base + h))

    row128 = pl.BlockSpec((1, LANES), lambda n, h, j: (0, 0))
    kern = functools.partial(_delta_prompt_kernel, n_heads=n_heads, chunk=chunk)
    return pl.pallas_call(
        kern, grid=grid,
        in_specs=[col_spec(0), col_spec(nhp), col_spec(2 * nhp), col_spec(3 * nhp),
                  pl.BlockSpec((rt, LANES), lambda n, h, j: (n * tiles + j, 0)),
                  cw_spec(0), cw_spec(nhp), cw_spec(2 * nhp), row128, row128, row128],
        out_specs=[pl.BlockSpec((rt, wblk), lambda n, h, j: (n * tiles + j, h)),
                   pl.BlockSpec((1, hp, LANES, LANES), lambda n, h, j: (n, h, 0, 0))],
        out_shape=[jax.ShapeDtypeStruct((n_seq * seq_len, n_heads * LANES), BF16),
                   jax.ShapeDtypeStruct((n_seq, n_heads, LANES, LANES), F32)],
        scratch_shapes=[pltpu.VMEM((3, rt, wblk), F32), pltpu.VMEM((hp, LANES, LANES), F32)],
        compiler_params=pltpu.CompilerParams(dimension_semantics=("arbitrary", "arbitrary", "arbitrary"),
                                             vmem_limit_bytes=VMEM_LIMIT_BYTES),
        name="delta_prompt",
    )(proj_a, proj_a, proj_a, proj_a, proj_ba, conv_w, conv_w, conv_w, alog_row, dtb_row, out_g)


def _conv3(h, hprev, cw, b, pos):
    acc = h * cw[2:3, :] + b
    for j in (1, 2):
        sh = jnp.where(pos < j, pltpu.roll(hprev, j, axis=0), pltpu.roll(h, j, axis=0))
        acc = acc + sh * cw[2 - j:3 - j, :]
    return acc


def _ffn_up_kernel(x_ref, wg_ref, wu_ref, cwg_ref, cwu_ref, bg_ref, bu_ref, pg_ref, pu_ref,
                   act_ref, tg_ref, tu_ref, hsg_ref, hsu_ref, wgb_ref, wub_ref, prevg_ref, prevu_ref,
                   *, tiles_per_seq, n_prompt_tiles, sample_rows, sample_len):
    m = pl.program_id(1)
    bm = x_ref.shape[0]

    @pl.when(m == 0)
    def _():
        wgb_ref[...] = wg_ref[...].astype(BF16)
        wub_ref[...] = wu_ref[...].astype(BF16)

    @pl.when(m < n_prompt_tiles)
    def _():
        x = x_ref[...]
        hg = _dot(x, wgb_ref[...])
        hu = _dot(x, wub_ref[...])
        first = (m % tiles_per_seq) == 0
        pos = lax.broadcasted_iota(jnp.int32, hg.shape, 0)
        pg = jnp.where(first, 0.0, prevg_ref[...])
        pu = jnp.where(first, 0.0, prevu_ref[...])
        g = _conv3(hg, pg, cwg_ref[...], bg_ref[...], pos)
        u = _conv3(hu, pu, cwu_ref[...], bu_ref[...], pos)
        act_ref[...] = (_gelu_tanh(g) * u).astype(act_ref.dtype)
        prevg_ref[...] = hg
        prevu_ref[...] = hu
        tg_ref[...] = hg[bm - 8:, :]
        tu_ref[...] = hu[bm - 8:, :]

    @pl.when(m == n_prompt_tiles)
    def _():
        x = x_ref[0:sample_rows, :]
        hg = _dot(x, wgb_ref[...])
        hu = _dot(x, wub_ref[...])
        pos = lax.broadcasted_iota(jnp.int32, hg.shape, 0) & (sample_len - 1)
        g = _conv3(hg, pg_ref[...], cwg_ref[...], bg_ref[...], pos)
        u = _conv3(hu, pu_ref[...], cwu_ref[...], bu_ref[...], pos)
        act_ref[0:sample_rows, :] = (_gelu_tanh(g) * u).astype(act_ref.dtype)
        hsg_ref[...] = hg
        hsu_ref[...] = hu


def _ffn_up(x, w_up, conv_w, conv_b, p_state, *, n_seq, seq_len, sample_rows, sample_len, bm, bn):
    t, d = x.shape
    f = w_up.shape[1] // 2
    tp = n_seq * seq_len
    assert seq_len % bm == 0 and f % bn == 0 and t == tp + sample_rows and sample_rows <= bm
    npt = tp // bm
    nj = f // bn
    grid = (nj, npt + 1)
    kern = functools.partial(_ffn_up_kernel, tiles_per_seq=seq_len // bm, n_prompt_tiles=npt,
                             sample_rows=sample_rows, sample_len=sample_len)

    def half(rows, base):
        return pl.BlockSpec((rows, bn), lambda j, m: (0, base + j))

    tail_spec = pl.BlockSpec((8, bn), lambda j, m: (jnp.minimum(m, npt - 1), j))
    return pl.pallas_call(
        kern, grid=grid,
        in_specs=[pl.BlockSpec((bm, d), lambda j, m: (m, 0)),
                  half(d, 0), half(d, nj), half(3, 0), half(3, nj), half(1, 0), half(1, nj),
                  half(sample_rows, 0), half(sample_rows, nj)],
        out_specs=[pl.BlockSpec((bm, bn), lambda j, m: (m, j)), tail_spec, tail_spec,
                   half(sample_rows, 0), half(sample_rows, 0)],
        out_shape=[jax.ShapeDtypeStruct((t, f), BF16),
                   jax.ShapeDtypeStruct((npt * 8, f), F32), jax.ShapeDtypeStruct((npt * 8, f), F32),
                   jax.ShapeDtypeStruct((sample_rows, f), F32), jax.ShapeDtypeStruct((sample_rows, f), F32)],
        scratch_shapes=[pltpu.VMEM((d, bn), BF16), pltpu.VMEM((d, bn), BF16),
                        pltpu.VMEM((bm, bn), F32), pltpu.VMEM((bm, bn), F32)],
        compiler_params=pltpu.CompilerParams(dimension_semantics=("arbitrary", "arbitrary"),
                                             vmem_limit_bytes=VMEM_LIMIT_BYTES),
        name="ffn_up",
    )(x, w_up, w_up, conv_w, conv_w, conv_b, conv_b, p_state, p_state)


def _merge_kernel(ya_ref, yb_ref, yc_ref, wa_ref, wb_ref, wc_ref, ga_ref, gb_ref, gc_ref, o_ref,
                  wab_ref, wbb_ref, wcb_ref):
    @pl.when(pl.program_id(1) == 0)
    def _():
        wab_ref[...] = wa_ref[...].astype(BF16)
        wbb_ref[...] = wb_ref[...].astype(BF16)
        wcb_ref[...] = wc_ref[...].astype(BF16)

    acc = _sigmoid(ga_ref[...]) * _dot(ya_ref[...], wab_ref[...])
    acc = acc + _sigmoid(gb_ref[...]) * _dot(yb_ref[...], wbb_ref[...])
    acc = acc + _sigmoid(gc_ref[...]) * _dot(yc_ref[...], wcb_ref[...])
    o_ref[...] = acc.astype(o_ref.dtype)


def _gated_merge(y_a, y_b, y_c, w_a, w_b, w_c, gate_logits, gate_col0, *, bm, bn):
    t, kdim = y_a.shape
    d = w_a.shape[1]
    assert t % bm == 0 and d % bn == 0 and gate_col0 % bn == 0
    nj = d // bn
    g0 = gate_col0 // bn
    y_spec = pl.BlockSpec((bm, kdim), lambda j, i: (i, 0))
    w_spec = pl.BlockSpec((kdim, bn), lambda j, i: (0, j))

    def g_spec(b):
        return pl.BlockSpec((bm, bn), lambda j, i: (i, g0 + b * nj + j))

    return pl.pallas_call(
        _merge_kernel, grid=(nj, t // bm),
        in_specs=[y_spec, y_spec, y_spec, w_spec, w_spec, w_spec, g_spec(0), g_spec(1), g_spec(2)],
        out_specs=pl.BlockSpec((bm, bn), lambda j, i: (i, j)),
        out_shape=jax.ShapeDtypeStruct((t, d), BF16),
        scratch_shapes=[pltpu.VMEM((kdim, bn), BF16)] * 3,
        compiler_params=pltpu.CompilerParams(dimension_semantics=("arbitrary", "arbitrary"),
                                             vmem_limit_bytes=VMEM_LIMIT_BYTES),
        name="gated_merge",
    )(y_a, y_b, y_c, w_a, w_b, w_c, gate_logits, gate_logits, gate_logits)


def _rms_norm(x, g):
    xf = x.astype(F32)
    y = xf * lax.rsqrt(jnp.mean(xf * xf, axis=-1, keepdims=True) + EPS)
    return y * g.astype(F32)


def _l2norm(x):
    return x * lax.rsqrt(jnp.sum(x * x, axis=-1, keepdims=True) + EPS)


def _causal_dwconv(x, buf, w):
    width = w.shape[0]
    L = x.shape[1]
    xp = jnp.concatenate([buf.astype(x.dtype), x], axis=1)
    y = xp[:, :L] * w[0]
    for j in range(1, width):
        y = y + xp[:, j:j + L] * w[j]
    return y, xp[:, L:]


def _gated_delta_rule(q, k, v, g, beta, s0):
    n, L, H, dk = q.shape
    dv = v.shape[-1]
    c = math.gcd(L, DELTA_CHUNK)
    nc = L // c

    def chunks(t):
        t = t.reshape((n, nc, c, H) + t.shape[3:])
        return jnp.moveaxis(t, (1, 3), (0, 2))

    qc, kc, vc, bc = chunks(q), chunks(k), chunks(v), chunks(beta)
    gc = jnp.cumsum(chunks(g), axis=-1)
    tri = jnp.tril(jnp.ones((c, c), bool))
    tri_strict = jnp.tril(jnp.ones((c, c), bool), -1)
    diff = gc[..., :, None] - gc[..., None, :]
    decay = jnp.where(tri, jnp.exp(jnp.where(tri, diff, 0.0)), 0.0)
    kk = jnp.einsum('znhrd,znhsd->znhrs', kc, kc)
    a_mat = jnp.where(tri_strict, bc[..., None] * kk * decay, 0.0)
    eye = jnp.eye(c, dtype=F32)
    rhs = jnp.concatenate([bc[..., None] * vc, (bc * jnp.exp(gc))[..., None] * kc], axis=-1)
    sol = lax.linalg.triangular_solve(eye + a_mat, rhs, left_side=True, lower=True, unit_diagonal=True)
    u_base, w_mat = sol[..., :dv], sol[..., dv:]
    qk = jnp.where(tri, jnp.einsum('znhrd,znhsd->znhrs', qc, kc) * decay, 0.0)
    q_dec = qc * jnp.exp(gc)[..., None]
    k_dec = kc * jnp.exp(gc[..., -1:] - gc)[..., None]
    g_last = jnp.exp(gc[..., -1])

    def step(s, xs):
        u_b, w_m, qk_m, q_d, k_d, g_l = xs
        u = u_b - jnp.einsum('nhrk,nhkv->nhrv', w_m, s)
        o = jnp.einsum('nhrk,nhkv->nhrv', q_d, s) + jnp.einsum('nhrs,nhsv->nhrv', qk_m, u)
        s = g_l[..., None, None] * s + jnp.einsum('nhsk,nhsv->nhkv', k_d, u)
        return s, o

    s_final, o = lax.scan(step, s0, (u_base, w_mat, qk, q_dec, k_dec, g_last))
    o = jnp.moveaxis(o, (0, 2), (1, 3)).reshape(n, L, H, dv)
    return o, s_final


def _delta_mixer(qkv, z, b_raw, a_raw, conv_buf, s0, conv_w, a_log, dt_bias, out_g):
    n, L, _ = qkv.shape
    y, conv_buf_new = _causal_dwconv(qkv, conv_buf, conv_w)
    y = jax.nn.silu(y.astype(F32))
    q, k, v = jnp.split(y, [A_QK, 2 * A_QK], axis=-1)
    q = _l2norm(q.reshape(n, L, A_HEADS, A_DK)) * (A_DK ** -0.5)
    k = _l2norm(k.reshape(n, L, A_HEADS, A_DK))
    v = v.reshape(n, L, A_HEADS, A_DV)
    beta = jax.nn.sigmoid(b_raw.astype(F32))
    g = -jnp.exp(a_log.astype(F32)) * jax.nn.softplus(a_raw.astype(F32) + dt_bias.astype(F32))
    o, s_new = _gated_delta_rule(q, k, v, g, beta, s0.astype(F32))
    o = o * lax.rsqrt(jnp.mean(o * o, axis=-1, keepdims=True) + EPS) * out_g.astype(F32)
    o = o * jax.nn.silu(z.astype(F32)).reshape(n, L, A_HEADS, A_DV)
    return o.reshape(n, L, A_VAL), conv_buf_new, s_new


def _sink_softmax(s, mask, sinks):
    s = jnp.where(mask, s, -jnp.inf)
    sk = sinks.astype(F32).reshape(B_KV_HEADS, B_GROUP, 1, 1)
    m = jnp.maximum(jnp.max(s, axis=-1, keepdims=True), sk)
    e = jnp.exp(s - m)
    return e / (jnp.sum(e, axis=-1, keepdims=True) + jnp.exp(sk - m))


def _window_attn_prompt(q, k, v, sinks):
    n, L = q.shape[:2]
    nb = L // WINDOW
    qb = q.reshape(n, nb, WINDOW, B_KV_HEADS, B_GROUP, B_HD)
    kb = k.reshape(n, nb, WINDOW, B_KV_HEADS, B_HD)
    vb = v.reshape(n, nb, WINDOW, B_KV_HEADS, B_HD)
    pad = ((0, 0), (1, 0), (0, 0), (0, 0), (0, 0))
    kk = jnp.concatenate([jnp.pad(kb, pad)[:, :-1], kb], axis=2)
    vv = jnp.concatenate([jnp.pad(vb, pad)[:, :-1], vb], axis=2)
    rel = (jnp.arange(WINDOW)[:, None] + WINDOW) - jnp.arange(2 * WINDOW)[None, :]
    band = (rel >= 0) & (rel < WINDOW)
    valid = (jnp.arange(nb)[:, None] * WINDOW - WINDOW + jnp.arange(2 * WINDOW)[None, :]) >= 0
    mask = (band[None] & valid[:, None, :])[:, None, None]
    s = jnp.einsum('nbqkgd,nbskd->nbkgqs', qb, kk).astype(F32) * (B_HD ** -0.5)
    pr = _sink_softmax(s, mask, sinks)
    o = jnp.einsum('nbkgqs,nbskd->nbqkgd', pr.astype(vv.dtype), vv)
    return o.reshape(n, L, B_Q)


def _window_attn_sample(q, k, v, k_buf, v_buf, sinks):
    n, L = q.shape[:2]
    wb = k_buf.shape[1]
    kk = jnp.concatenate([k_buf.astype(k.dtype), k], axis=1)
    vv = jnp.concatenate([v_buf.astype(v.dtype), v], axis=1)
    rel = (wb + jnp.arange(L))[:, None] - jnp.arange(wb + L)[None, :]
    mask = (rel >= 0) & (rel < WINDOW)
    s = jnp.einsum('nqkgd,nskd->nkgqs', q, kk).astype(F32) * (B_HD ** -0.5)
    pr = _sink_softmax(s, mask, sinks)
    o = jnp.einsum('nkgqs,nskd->nqkgd', pr.astype(vv.dtype), vv).reshape(n, L, B_Q)
    return o, kk[:, L:], vv[:, L:]


def _chunk_mlp(uv, ln_g, ln_b, w_s, b_s):
    n, L, _ = uv.shape
    u, v = jnp.split(jax.nn.gelu(uv, approximate=True), 2, axis=-1)
    vf = v.astype(F32)
    mu = jnp.mean(vf, axis=-1, keepdims=True)
    var = jnp.mean(jnp.square(vf - mu), axis=-1, keepdims=True)
    vn = (vf - mu) * lax.rsqrt(var + EPS) * ln_g.astype(F32) + ln_b.astype(F32)
    nc = -(-L // C_CHUNK)
    vp = jnp.pad(vn, ((0, 0), (0, nc * C_CHUNK - L), (0, 0))).reshape(n, nc, C_CHUNK, C_GROUPS, C_GC)
    w_causal = jnp.where(jnp.tril(jnp.ones((C_CHUNK, C_CHUNK), bool)), w_s, 0.0)
    mixed = jnp.einsum('gts,ncsgd->nctgd', w_causal, vp) + b_s.T[:, :, None]
    mixed = mixed.reshape(n, nc * C_CHUNK, C_HALF)[:, :L]
    return u * mixed, vn


BM = 1056
BN = 512
FFN_BM = 1024
FFN_BN = 256
OFF_BA = A_CONV_CH + A_VAL
OFF_REST = OFF_BA + 2 * A_HEADS
OFF_GATES = B_Q + 2 * B_KV + 2 * C_HALF


def _attn_cmlp(proj_r, n, L, prm, i, past):
    o = 0
    b_q = proj_r[:, o:o + B_Q]; o += B_Q
    b_k = proj_r[:, o:o + B_KV]; o += B_KV
    b_v = proj_r[:, o:o + B_KV]; o += B_KV
    c_uv = proj_r[:, o:o + 2 * C_HALF].reshape(n, L, 2 * C_HALF)
    q = b_q.reshape(n, L, B_KV_HEADS, B_GROUP, B_HD)
    k = b_k.reshape(n, L, B_KV_HEADS, B_HD)
    v = b_v.reshape(n, L, B_KV_HEADS, B_HD)
    if past is None:
        y_b = _window_attn_prompt(q, k, v, prm['b_sinks'][i])
        wb = min(WINDOW, L)
        k_new, v_new = k[:, L - wb:], v[:, L - wb:]
    else:
        y_b, k_new, v_new = _window_attn_sample(q, k, v, past[0], past[1], prm['b_sinks'][i])
    y_c, c_v = _chunk_mlp(c_uv, prm['c_ln_g'][i], prm['c_ln_b'][i], prm['c_w_s'][i], prm['c_b_s'][i])
    t = n * L
    return y_b.reshape(t, B_Q), y_c.reshape(t, C_HALF), k_new, v_new, c_v


def kernel(x_prompt, x_sample, p_prompt, p_sample, state_a_conv, state_delta, cache_win_k, cache_win_v, state_ffn_conv, norm_mix_pre, norm_mix_post, norm_ffn_pre, norm_ffn_post, norm_ple, w_in, a_conv_w, a_log, a_dt_bias, a_out_norm, b_sinks, c_ln_g, c_ln_b, c_w_s, c_b_s, w_br_a, w_br_b, w_br_c, w_o, w_up, ffn_conv_w, ffn_conv_b, w_down, w_ple, w_ple_gate):
    prm = {'b_sinks': b_sinks, 'c_ln_g': c_ln_g, 'c_ln_b': c_ln_b, 'c_w_s': c_w_s, 'c_b_s': c_b_s}
    nb, ls = x_prompt.shape[:2]
    ns, lq = x_sample.shape[:2]
    tp, ts = nb * ls, ns * lq
    ple = p_prompt.shape[-1]
    h = jnp.concatenate([x_prompt.reshape(tp, D_MODEL), x_sample.reshape(ts, D_MODEL)], axis=0)
    st_p, st_s = [], []
    for i in range(DEPTH):
        xn = _rms_norm(h, norm_mix_pre[i]).astype(BF16)
        w_ba = jnp.pad(w_in[i][:, OFF_BA:OFF_REST], ((0, 0), (0, LANES - 2 * A_HEADS)))
        w_rest = w_in[i][:, OFF_REST:]
        proj_a = _matmul(xn, w_in[i], bm=BM, bn=BN, n_cols=OFF_BA)
        proj_ba = _matmul(xn, w_ba, bm=BM, bn=LANES)
        proj_r = _matmul(xn, w_rest, bm=BM, bn=BN)
        alog_row = jnp.zeros((1, LANES), F32).at[0, A_HEADS:2 * A_HEADS].set(a_log[i])
        dtb_row = jnp.zeros((1, LANES), F32).at[0, A_HEADS:2 * A_HEADS].set(a_dt_bias[i])
        ya_p, s_p = _delta_prompt(proj_a, proj_ba, a_conv_w[i], alog_row, dtb_row, a_out_norm[i].reshape(1, A_DV),
                                  n_seq=nb, seq_len=ls, n_heads=A_HEADS, chunk=math.gcd(ls, DELTA_CHUNK))
        abuf_p = proj_a[:tp, :A_CONV_CH].reshape(nb, ls, A_CONV_CH)[:, ls - (A_CONV - 1):]
        pa_s, pba_s = proj_a[tp:], proj_ba[tp:]
        ya_s, abuf_s, s_s = _delta_mixer(
            pa_s[:, :A_CONV_CH].reshape(ns, lq, A_CONV_CH), pa_s[:, A_CONV_CH:].reshape(ns, lq, A_VAL),
            pba_s[:, :A_HEADS].reshape(ns, lq, A_HEADS), pba_s[:, A_HEADS:2 * A_HEADS].reshape(ns, lq, A_HEADS),
            state_a_conv[i], state_delta[i], a_conv_w[i], a_log[i], a_dt_bias[i], a_out_norm[i])
        yb_p, yc_p, k_p, v_p, _ = _attn_cmlp(proj_r[:tp, :OFF_GATES], nb, ls, prm, i, None)
        yb_s, yc_s, k_s, v_s, cv_s = _attn_cmlp(proj_r[tp:, :OFF_GATES], ns, lq, prm, i,
                                                (cache_win_k[i], cache_win_v[i]))
        y_a = jnp.concatenate([ya_p, ya_s.reshape(ts, A_VAL).astype(BF16)])
        y_b = jnp.concatenate([yb_p, yb_s]).astype(BF16)
        y_c = jnp.concatenate([yc_p, yc_s]).astype(BF16)
        merged = _gated_merge(y_a, y_b, y_c, w_br_a[i], w_br_b[i], w_br_c[i], proj_r, OFF_GATES, bm=BM, bn=BN)
        h = h + _rms_norm(_matmul(merged, w_o[i], bm=BM, bn=BN), norm_mix_post[i])
        xf = _rms_norm(h, norm_ffn_pre[i]).astype(BF16)
        p_state = jnp.roll(jnp.pad(state_ffn_conv[i], ((0, 0), (lq - (FFN_CONV - 1), 0), (0, 0))), -1, axis=0)
        act, tail_g, tail_u, hs_g, hs_u = _ffn_up(
            xf, w_up[i], ffn_conv_w[i], ffn_conv_b[i].reshape(1, 2 * D_FF), p_state.reshape(ts, 2 * D_FF),
            n_seq=nb, seq_len=ls, sample_rows=ts, sample_len=lq, bm=FFN_BM, bn=FFN_BN)
        tails = jnp.concatenate([tail_g, tail_u], axis=1).reshape(nb, ls // FFN_BM, 8, 2 * D_FF)
        fb_p = tails[:, -1, 8 - (FFN_CONV - 1):]
        fb_s = jnp.concatenate([hs_g, hs_u], axis=1).reshape(ns, lq, 2 * D_FF)[:, lq - (FFN_CONV - 1):]
        f = None
        for c in range(D_FF // D_MODEL):
            f = _matmul(act, w_down[i], bm=BM, bn=BN, k_chunk=c, bk=D_MODEL, acc=f)
        h = h + _rms_norm(f, norm_ffn_post[i])
        hn = _rms_norm(h, norm_ple[i]).astype(BF16)
        gate = jax.nn.sigmoid(_matmul(hn, w_ple_gate[i], bm=BM, bn=BN))
        p_i = jnp.concatenate([p_prompt[i].reshape(tp, ple), p_sample[i].reshape(ts, ple)]).astype(BF16)
        h = h + gate * _matmul(p_i, w_ple[i], bm=BM, bn=BN)
        st_p.append((abuf_p, s_p, k_p, v_p, fb_p))
        st_s.append((abuf_s, s_s, k_s, v_s, fb_s, cv_s))

    def stack(states, j):
        return jnp.stack([s[j] for s in states])

    y_prompt = h[:tp].reshape(nb, ls, D_MODEL)
    y_sample = h[tp:].reshape(ns, lq, D_MODEL)
    return (y_prompt, y_sample,
            stack(st_p, 0), stack(st_p, 1), stack(st_p, 2), stack(st_p, 3), stack(st_p, 4),
            stack(st_s, 0), stack(st_s, 1), stack(st_s, 2), stack(st_s, 3), stack(st_s, 4),
            stack(st_s, 5))
```

```python
import functools
import math

import jax
import jax.numpy as jnp
from jax import lax
from jax.experimental import pallas as pl
from jax.experimental.pallas import tpu as pltpu

D_MODEL = 4096
DEPTH = 4
A_HEADS = 8
A_DK = 128
A_DV = 128
A_QK = A_HEADS * A_DK
A_VAL = A_HEADS * A_DV
A_CONV = 4
A_CONV_CH = 2 * A_QK + A_VAL
DELTA_CHUNK = 64
B_HEADS = 16
B_KV_HEADS = 4
B_GROUP = B_HEADS // B_KV_HEADS
B_HD = 64
B_Q = B_HEADS * B_HD
B_KV = B_KV_HEADS * B_HD
WINDOW = 128
C_GROUPS = 8
C_GC = 128
C_HALF = C_GROUPS * C_GC
C_CHUNK = 128
D_FF = 3 * D_MODEL
FFN_CONV = 3
N_BRANCH = 3
EPS = 1e-6

F32 = jnp.float32
BF16 = jnp.bfloat16
LANES = 128

VMEM_LIMIT_BYTES = 56 * 1024 * 1024

DELTA_RT = 256
DELTA_HP = 2
NEUMANN_BLOCK = 16
NORM_ROWS = 264
CAST_ROWS = 512

BM = 1056
BN = 512
FFN_BM = 1024
FFN_BN = 256
OFF_BA = A_CONV_CH + A_VAL
SHIFT = 2 * A_HEADS
OFF_BQ = OFF_BA
OFF_KV = OFF_BQ + B_Q
OFF_CUV = OFF_KV + 2 * B_KV
OFF_GATES = OFF_CUV + 2 * C_HALF


def _params(n_axes):
    return pltpu.CompilerParams(dimension_semantics=("arbitrary",) * n_axes, vmem_limit_bytes=VMEM_LIMIT_BYTES)


def _sigmoid(x):
    return 1.0 / (1.0 + jnp.exp(-x))


def _softplus(x):
    return jnp.maximum(x, 0.0) + jnp.log(1.0 + jnp.exp(-jnp.abs(x)))


def _gelu_tanh(x):
    return 0.5 * x * (1.0 + jnp.tanh(0.7978845608028654 * (x + 0.044715 * (x * x * x))))


def _rms(x, g):
    return x * lax.rsqrt(jnp.mean(x * x, axis=-1, keepdims=True) + EPS) * g


def _dot(a, b):
    return jnp.dot(a, b, preferred_element_type=F32)


def _dot_nt(a, b):
    return lax.dot_general(a, b, (((1,), (1,)), ((), ())), preferred_element_type=F32)


def _dot_tn(a, b):
    return lax.dot_general(a, b, (((0,), (0,)), ((), ())), preferred_element_type=F32)


def _split2(x):
    hi = x.astype(BF16)
    lo = (x - hi.astype(F32)).astype(BF16)
    return hi, lo


def _dot3s(a, b):
    ah, al = a
    bh, bl = b
    return _dot(ah, bh) + (_dot(ah, bl) + _dot(al, bh))


def _mm_kernel(x_ref, w_ref, o_ref, wbf_ref):
    @pl.when(pl.program_id(1) == 0)
    def _():
        wbf_ref[...] = w_ref[...].astype(BF16)

    o_ref[...] = _dot(x_ref[...], wbf_ref[...])


def _mm_acc_kernel(x_ref, w_ref, a_ref, o_ref, wbf_ref):
    @pl.when(pl.program_id(1) == 0)
    def _():
        wbf_ref[...] = w_ref[...].astype(BF16)

    o_ref[...] = a_ref[...] + _dot(x_ref[...], wbf_ref[...])


def _matmul(x, w, layer, *, bm, bn, n_cols=None, col0=0, k_chunk=0, bk=None, acc=None):
    m = x.shape[0]
    bk = w.shape[1] if bk is None else bk
    n_cols = w.shape[2] if n_cols is None else n_cols
    assert m % bm == 0 and n_cols % bn == 0 and col0 % bn == 0
    cb0 = col0 // bn
    grid = (n_cols // bn, m // bm)
    x_spec = pl.BlockSpec((bm, bk), lambda j, i: (i, k_chunk))
    w_spec = pl.BlockSpec((None, bk, bn), lambda j, i: (layer, k_chunk, cb0 + j))
    o_spec = pl.BlockSpec((bm, bn), lambda j, i: (i, j))
    out_shape = jax.ShapeDtypeStruct((m, n_cols), F32)
    scratch = [pltpu.VMEM((bk, bn), BF16)]
    if acc is None:
        return pl.pallas_call(_mm_kernel, grid=grid, in_specs=[x_spec, w_spec], out_specs=o_spec,
                              out_shape=out_shape, scratch_shapes=scratch, compiler_params=_params(2),
                              name="proj")(x, w)
    return pl.pallas_call(_mm_acc_kernel, grid=grid, in_specs=[x_spec, w_spec, o_spec], out_specs=o_spec,
                          out_shape=out_shape, scratch_shapes=scratch, compiler_params=_params(2),
                          input_output_aliases={2: 0}, name="proj_acc")(x, w, acc)


def _mm_shift_kernel(x_ref, wa_ref, wb_ref, o_ref, wbf_ref, *, shift):
    @pl.when(pl.program_id(1) == 0)
    def _():
        k, bn = wa_ref.shape
        for r0 in range(0, k, CAST_ROWS):
            rs = slice(r0, r0 + CAST_ROWS)
            w = jnp.concatenate([wa_ref[rs, :], wb_ref[rs, :]], axis=1)
            wbf_ref[rs, :] = pltpu.roll(w, w.shape[1] - shift, axis=1)[:, :bn].astype(BF16)

    o_ref[...] = _dot(x_ref[...], wbf_ref[...])


def _matmul_shifted(x, w, layer, *, bm, bn, col0, n_cols, shift):
    m, k = x.shape
    assert m % bm == 0 and n_cols % bn == 0 and col0 % bn == 0 and 0 < shift < LANES and k % CAST_ROWS == 0
    cb0 = col0 // bn
    lpb = bn // LANES
    return pl.pallas_call(
        functools.partial(_mm_shift_kernel, shift=shift), grid=(n_cols // bn, m // bm),
        in_specs=[pl.BlockSpec((bm, k), lambda j, i: (i, 0)),
                  pl.BlockSpec((None, k, bn), lambda j, i: (layer, 0, cb0 + j)),
                  pl.BlockSpec((None, k, LANES), lambda j, i: (layer, 0, (cb0 + j + 1) * lpb))],
        out_specs=pl.BlockSpec((bm, bn), lambda j, i: (i, j)),
        out_shape=jax.ShapeDtypeStruct((m, n_cols), F32),
        scratch_shapes=[pltpu.VMEM((k, bn), BF16)],
        compiler_params=_params(2), name="proj_shift",
    )(x, w, w)


def _norm_kernel(h_ref, g_ref, xn_ref):
    xn_ref[...] = _rms(h_ref[...], g_ref[...]).astype(xn_ref.dtype)


def _resnorm_kernel(h_ref, y_ref, gp_ref, gn_ref, ho_ref, xn_ref):
    h = h_ref[...] + _rms(y_ref[...], gp_ref[...])
    ho_ref[...] = h
    xn_ref[...] = _rms(h, gn_ref[...]).astype(xn_ref.dtype)


def _rms_norm_bf16(h, g):
    t, d = h.shape
    assert t % NORM_ROWS == 0
    row = pl.BlockSpec((NORM_ROWS, d), lambda r: (r, 0))
    vec = pl.BlockSpec((1, d), lambda r: (0, 0))
    return pl.pallas_call(_norm_kernel, grid=(t // NORM_ROWS,), in_specs=[row, vec], out_specs=row,
                          out_shape=jax.ShapeDtypeStruct((t, d), BF16), compiler_params=_params(1),
                          name="rms_norm")(h, g.reshape(1, d))


def _residual_norm(h, y, g_post, g_next):
    t, d = h.shape
    assert t % NORM_ROWS == 0
    row = pl.BlockSpec((NORM_ROWS, d), lambda r: (r, 0))
    vec = pl.BlockSpec((1, d), lambda r: (0, 0))
    return pl.pallas_call(
        _resnorm_kernel, grid=(t // NORM_ROWS,), in_specs=[row, row, vec, vec], out_specs=[row, row],
        out_shape=[jax.ShapeDtypeStruct((t, d), F32), jax.ShapeDtypeStruct((t, d), BF16)],
        input_output_aliases={0: 0}, compiler_params=_params(1), name="residual_norm",
    )(h, y, g_post.reshape(1, d), g_next.reshape(1, d))


def _unit_lower_inverse(a, rx, c):
    eye = (rx == 0).astype(F32)
    blk = min(NEUMANN_BLOCK, c)
    n = -jnp.where(rx < blk, a, 0.0)
    p = eye + n
    pw = n
    for _ in range(int(math.log2(blk)) - 1):
        pws = _split2(pw)
        pw = _dot3s(pws, pws)
        p = p + _dot3s(_split2(p), _split2(pw))
    while blk < c:
        e = jnp.where((rx >= blk) & (rx < 2 * blk), a, 0.0)
        ps = _split2(p)
        p = p - _dot3s(ps, _split2(_dot3s(_split2(e), ps)))
        blk *= 2
    return p


def _conv_silu(x, xprev, cw, seg):
    pos = lax.broadcasted_iota(jnp.int32, x.shape, 0) & (seg - 1)
    acc = x * cw[3:4, :]
    for j in range(1, 4):
        sh = jnp.where(pos < j, pltpu.roll(xprev, j, axis=0), pltpu.roll(x, j, axis=0))
        acc = acc + sh * cw[3 - j:4 - j, :]
    return acc * _sigmoid(acc)


def _delta_tile_prep(q_raw, k_raw, v_raw, pq, pk, pv, cwq, cwk, cwv, beta_col, g_col, c, seg):
    rt = q_raw.shape[0]
    q = _conv_silu(q_raw, pq, cwq, seg)
    k = _conv_silu(k_raw, pk, cwk, seg)
    v = _conv_silu(v_raw, pv, cwv, seg)
    q = q * (lax.rsqrt(jnp.sum(q * q, axis=-1, keepdims=True) + EPS) * (A_DK ** -0.5))
    k = k * lax.rsqrt(jnp.sum(k * k, axis=-1, keepdims=True) + EPS)

    r_i = lax.broadcasted_iota(jnp.int32, (rt, rt), 0)
    c_i = lax.broadcasted_iota(jnp.int32, (rt, rt), 1)
    rx = r_i ^ c_i
    same = rx < c
    lower = same & (c_i <= r_i)
    strict = same & (c_i < r_i)

    g1 = g_col.astype(BF16)
    r1 = g_col - g1.astype(F32)
    g2 = r1.astype(BF16)
    g3 = (r1 - g2.astype(F32)).astype(BF16)
    lmat = lower.astype(BF16)
    umat = (same & (r_i <= c_i)).astype(BF16)
    ones = jnp.ones((rt, rt), BF16)
    gc = None
    gr = None
    for gp in (g1, g2, g3):
        a = _dot(lmat, jnp.broadcast_to(gp, (rt, LANES)))
        b = _dot(ones, jnp.broadcast_to(gp, (rt, rt)) * umat)
        gc = a if gc is None else gc + a
        gr = b if gr is None else gr + b
    gc_full = jnp.concatenate([gc] * (rt // LANES), axis=1)
    decay = jnp.where(lower, jnp.exp(jnp.where(lower, gc_full - gr, 0.0)), 0.0)

    kb = k.astype(BF16)
    kk = _dot_nt(kb, kb)
    a_mat = jnp.where(strict, beta_col * kk * decay, 0.0)
    t_inv = _unit_lower_inverse(a_mat, rx, c)
    rhs = jnp.concatenate([beta_col * v, (beta_col * jnp.exp(gc)) * k], axis=1)
    sol = _dot3s(_split2(t_inv), _split2(rhs))
    u_base = sol[:, :LANES]
    w_mat = sol[:, LANES:]
    qk = jnp.where(lower, _dot_nt(q.astype(BF16), kb) * decay, 0.0)
    q_dec = q * jnp.exp(gc)
    return k, gc, u_base, w_mat, qk, q_dec


def _gated_out_norm(o, z, og):
    return _rms(o, og) * (z * _sigmoid(z))


def _head_cols(ba, alog_row, dtb_row, h, n_heads):
    lane = lax.broadcasted_iota(jnp.int32, ba.shape, 1)
    sig = _sigmoid(ba)
    g_all = -jnp.exp(alog_row) * _softplus(ba + dtb_row)
    beta_col = jnp.sum(jnp.where(lane == h, sig, 0.0), axis=1, keepdims=True)
    g_col = jnp.sum(jnp.where(lane == n_heads + h, g_all, 0.0), axis=1, keepdims=True)
    return beta_col, g_col


def _delta_prompt_kernel(q_ref, k_ref, v_ref, z_ref, ba_ref, cwq_ref, cwk_ref, cwv_ref, alog_ref, dtb_ref, og_ref,
                         y_ref, sfin_ref, xprev_ref, s_ref, *, n_heads, chunk):
    j = pl.program_id(2)
    hp = pl.program_id(1)
    rt = q_ref.shape[0]

    @pl.when(j == 0)
    def _():
        xprev_ref[...] = jnp.zeros_like(xprev_ref)
        s_ref[...] = jnp.zeros_like(s_ref)

    ba = ba_ref[...]
    for hh in range(DELTA_HP):
        cs = slice(hh * LANES, (hh + 1) * LANES)
        beta_col, g_col = _head_cols(ba, alog_ref[...], dtb_ref[...], hp * DELTA_HP + hh, n_heads)
        q_raw, k_raw, v_raw = q_ref[:, cs], k_ref[:, cs], v_ref[:, cs]
        k, gc, u_base, w_mat, qk, q_dec = _delta_tile_prep(
            q_raw, k_raw, v_raw, xprev_ref[0, :, cs], xprev_ref[1, :, cs], xprev_ref[2, :, cs],
            cwq_ref[:, cs], cwk_ref[:, cs], cwv_ref[:, cs], beta_col, g_col, chunk, rt)
        xprev_ref[0, :, cs] = q_raw
        xprev_ref[1, :, cs] = k_raw
        xprev_ref[2, :, cs] = v_raw
        s = s_ref[hh]
        us, os_ = [], []
        for cc in range(rt // chunk):
            rs = slice(cc * chunk, (cc + 1) * chunk)
            sb = s.astype(BF16)
            u_c = u_base[rs] - _dot(w_mat[rs].astype(BF16), sb)
            os_.append(_dot(q_dec[rs].astype(BF16), sb))
            gl = gc[(cc + 1) * chunk - 1:(cc + 1) * chunk, :]
            k_dec = k[rs] * jnp.exp(gl - gc[rs])
            s = jnp.exp(gl) * s + _dot_tn(k_dec.astype(BF16), u_c.astype(BF16))
            us.append(u_c)
        s_ref[hh] = s
        u_all = jnp.concatenate(us, axis=0)
        o = jnp.concatenate(os_, axis=0) + _dot(qk.astype(BF16), u_all.astype(BF16))
        y_ref[:, cs] = _gated_out_norm(o, z_ref[:, cs], og_ref[...]).astype(y_ref.dtype)

    @pl.when(j == pl.num_programs(2) - 1)
    def _():
        sfin_ref[0] = s_ref[...]


def _delta_prompt(proj_a, proj_ba, conv_w, alog_row, dtb_row, out_g, *, n_seq, seq_len, n_heads, chunk):
    rt, hp = DELTA_RT, DELTA_HP
    assert seq_len % rt == 0 and rt % chunk == 0 and n_heads % hp == 0
    tiles = seq_len // rt
    nhp = n_heads // hp
    wblk = hp * LANES

    def col_spec(base):
        return pl.BlockSpec((rt, wblk), lambda n, h, j: (n * tiles + j, base + h))

    def cw_spec(base):
        return pl.BlockSpec((4, wblk), lambda n, h, j: (0, base + h))

    row128 = pl.BlockSpec((1, LANES), lambda n, h, j: (0, 0))
    kern = functools.partial(_delta_prompt_kernel, n_heads=n_heads, chunk=chunk)
    return pl.pallas_call(
        kern, grid=(n_seq, nhp, tiles),
        in_specs=[col_spec(0), col_spec(nhp), col_spec(2 * nhp), col_spec(3 * nhp),
                  pl.BlockSpec((rt, LANES), lambda n, h, j: (n * tiles + j, 0)),
                  cw_spec(0), cw_spec(nhp), cw_spec(2 * nhp), row128, row128, row128],
        out_specs=[pl.BlockSpec((rt, wblk), lambda n, h, j: (n * tiles + j, h)),
                   pl.BlockSpec((1, hp, LANES, LANES), lambda n, h, j: (n, h, 0, 0))],
        out_shape=[jax.ShapeDtypeStruct((n_seq * seq_len, n_heads * LANES), BF16),
                   jax.ShapeDtypeStruct((n_seq, n_heads, LANES, LANES), F32)],
        scratch_shapes=[pltpu.VMEM((3, rt, wblk), F32), pltpu.VMEM((hp, LANES, LANES), F32)],
        compiler_params=_params(3), name="delta_prompt",
    )(proj_a, proj_a, proj_a, proj_a, proj_ba, conv_w, conv_w, conv_w, alog_row, dtb_row, out_g)


def _attn_prompt_kernel(q_ref, kc_ref, kp_ref, vc_ref, vp_ref, sk_ref, o_ref, *, blocks_per_seq):
    b = pl.program_id(0) % blocks_per_seq
    w = q_ref.shape[0]
    r_i = lax.broadcasted_iota(jnp.int32, (w, 2 * w), 0)
    c_i = lax.broadcasted_iota(jnp.int32, (w, 2 * w), 1)
    mask = (c_i > r_i) & (c_i <= r_i + w) & ((b > 0) | (c_i >= w))
    scale = B_HD ** -0.5
    for kvh in range(B_KV_HEADS):
        ks = slice(kvh * B_HD, (kvh + 1) * B_HD)
        kk = jnp.concatenate([kp_ref[:, ks], kc_ref[:, ks]], axis=0).astype(BF16)
        vv = jnp.concatenate([vp_ref[:, ks], vc_ref[:, ks]], axis=0).astype(BF16)
        for g in range(B_GROUP):
            h = kvh * B_GROUP + g
            hs = slice(h * B_HD, (h + 1) * B_HD)
            s = _dot_nt(q_ref[:, hs].astype(BF16), kk) * scale
            s = jnp.where(mask, s, -jnp.inf)
            sink = sk_ref[h]
            m = jnp.maximum(jnp.max(s, axis=-1, keepdims=True), sink)
            e = jnp.exp(s - m)
            p = e / (jnp.sum(e, axis=-1, keepdims=True) + jnp.exp(sink - m))
            o_ref[:, hs] = _dot(p.astype(BF16), vv).astype(o_ref.dtype)


def _attn_prompt(q_arr, kv_arr, sinks, *, n_seq, seq_len):
    w = WINDOW
    bps = seq_len // w
    cur = lambda r: (r, 0)
    return pl.pallas_call(
        functools.partial(_attn_prompt_kernel, blocks_per_seq=bps), grid=(n_seq * bps,),
        in_specs=[pl.BlockSpec((w, B_Q), cur),
                  pl.BlockSpec((w, B_KV), cur), pl.BlockSpec((w, B_KV), lambda r: (jnp.maximum(r - 1, 0), 0)),
                  pl.BlockSpec((w, B_KV), lambda r: (r, 1)), pl.BlockSpec((w, B_KV), lambda r: (jnp.maximum(r - 1, 0), 1)),
                  pl.BlockSpec(memory_space=pltpu.SMEM)],
        out_specs=pl.BlockSpec((w, B_Q), cur),
        out_shape=jax.ShapeDtypeStruct((n_seq * seq_len, B_Q), BF16),
        compiler_params=_params(1), name="attn_prompt",
    )(q_arr, kv_arr, kv_arr, kv_arr, kv_arr, sinks)


def _cmlp_kernel(u_ref, v_ref, lg_ref, lb_ref, ws_ref, bs_ref, o_ref):
    v = _gelu_tanh(v_ref[...])
    mu = jnp.mean(v, axis=-1, keepdims=True)
    d = v - mu
    var = jnp.mean(d * d, axis=-1, keepdims=True)
    vn = d * lax.rsqrt(var + EPS) * lg_ref[...] + lb_ref[...]
    c = v.shape[0]
    tril = lax.broadcasted_iota(jnp.int32, (c, c), 1) <= lax.broadcasted_iota(jnp.int32, (c, c), 0)
    for g in range(C_GROUPS):
        gs = slice(g * C_GC, (g + 1) * C_GC)
        wc = jnp.where(tril, ws_ref[g], 0.0).astype(BF16)
        mixed = _dot(wc, vn[:, gs].astype(BF16)) + bs_ref[:, g:g + 1]
        o_ref[:, gs] = (_gelu_tanh(u_ref[:, gs]) * mixed).astype(o_ref.dtype)


def _cmlp_prompt(uv_arr, ln_g, ln_b, w_s, b_s_t, *, n_rows):
    c = C_CHUNK
    return pl.pallas_call(
        _cmlp_kernel, grid=(n_rows // c,),
        in_specs=[pl.BlockSpec((c, C_HALF), lambda r: (r, 0)), pl.BlockSpec((c, C_HALF), lambda r: (r, 1)),
                  pl.BlockSpec((1, C_HALF), lambda r: (0, 0)), pl.BlockSpec((1, C_HALF), lambda r: (0, 0)),
                  pl.BlockSpec((C_GROUPS, c, c), lambda r: (0, 0, 0)), pl.BlockSpec((c, C_GROUPS), lambda r: (0, 0))],
        out_specs=pl.BlockSpec((c, C_HALF), lambda r: (r, 0)),
        out_shape=jax.ShapeDtypeStruct((n_rows, C_HALF), BF16),
        compiler_params=_params(1), name="cmlp_prompt",
    )(uv_arr, uv_arr, ln_g, ln_b, w_s, b_s_t)


def _merge_kernel(ya_ref, yb_ref, yc_ref, wa_ref, wb_ref, wc_ref, ga_ref, gb_ref, gc_ref, o_ref,
                  wab_ref, wbb_ref, wcb_ref):
    @pl.when(pl.program_id(1) == 0)
    def _():
        wab_ref[...] = wa_ref[...].astype(BF16)
        wbb_ref[...] = wb_ref[...].astype(BF16)
        wcb_ref[...] = wc_ref[...].astype(BF16)

    acc = _sigmoid(ga_ref[...]) * _dot(ya_ref[...], wab_ref[...])
    acc = acc + _sigmoid(gb_ref[...]) * _dot(yb_ref[...], wbb_ref[...])
    acc = acc + _sigmoid(gc_ref[...]) * _dot(yc_ref[...], wcb_ref[...])
    o_ref[...] = acc.astype(o_ref.dtype)


def _gated_merge(y_a, y_b, y_c, w_a, w_b, w_c, layer, gate_logits, *, bm, bn):
    t, kdim = y_a.shape
    d = w_a.shape[2]
    assert t % bm == 0 and d % bn == 0
    nj = d // bn
    y_spec = pl.BlockSpec((bm, kdim), lambda j, i: (i, 0))
    w_spec = pl.BlockSpec((None, kdim, bn), lambda j, i: (layer, 0, j))

    def g_spec(b):
        return pl.BlockSpec((bm, bn), lambda j, i: (i, b * nj + j))

    return pl.pallas_call(
        _merge_kernel, grid=(nj, t // bm),
        in_specs=[y_spec, y_spec, y_spec, w_spec, w_spec, w_spec, g_spec(0), g_spec(1), g_spec(2)],
        out_specs=pl.BlockSpec((bm, bn), lambda j, i: (i, j)),
        out_shape=jax.ShapeDtypeStruct((t, d), BF16),
        scratch_shapes=[pltpu.VMEM((kdim, bn), BF16)] * 3,
        compiler_params=_params(2), name="gated_merge",
    )(y_a, y_b, y_c, w_a, w_b, w_c, gate_logits, gate_logits, gate_logits)


def _conv3(h, hprev, cw, b, pos):
    acc = h * cw[2:3, :] + b
    for j in (1, 2):
        sh = jnp.where(pos < j, pltpu.roll(hprev, j, axis=0), pltpu.roll(h, j, axis=0))
        acc = acc + sh * cw[2 - j:3 - j, :]
    return acc


def _ffn_up_kernel(x_ref, wg_ref, wu_ref, cwg_ref, cwu_ref, bg_ref, bu_ref, pg_ref, pu_ref,
                   act_ref, tg_ref, tu_ref, hsg_ref, hsu_ref, wgb_ref, wub_ref, prevg_ref, prevu_ref,
                   *, tiles_per_seq, n_prompt_tiles, sample_rows, sample_len):
    m = pl.program_id(1)
    bm = x_ref.shape[0]

    @pl.when(m == 0)
    def _():
        wgb_ref[...] = wg_ref[...].astype(BF16)
        wub_ref[...] = wu_ref[...].astype(BF16)

    @pl.when(m < n_prompt_tiles)
    def _():
        x = x_ref[...]
        hg = _dot(x, wgb_ref[...])
        hu = _dot(x, wub_ref[...])
        first = (m % tiles_per_seq) == 0
        pos = lax.broadcasted_iota(jnp.int32, hg.shape, 0)
        pg = jnp.where(first, 0.0, prevg_ref[...])
        pu = jnp.where(first, 0.0, prevu_ref[...])
        g = _conv3(hg, pg, cwg_ref[...], bg_ref[...], pos)
        u = _conv3(hu, pu, cwu_ref[...], bu_ref[...], pos)
        act_ref[...] = (_gelu_tanh(g) * u).astype(act_ref.dtype)
        prevg_ref[...] = hg
        prevu_ref[...] = hu
        tg_ref[...] = hg[bm - 8:, :]
        tu_ref[...] = hu[bm - 8:, :]

    @pl.when(m == n_prompt_tiles)
    def _():
        x = x_ref[0:sample_rows, :]
        hg = _dot(x, wgb_ref[...])
        hu = _dot(x, wub_ref[...])
        pos = lax.broadcasted_iota(jnp.int32, hg.shape, 0) & (sample_len - 1)
        g = _conv3(hg, pg_ref[...], cwg_ref[...], bg_ref[...], pos)
        u = _conv3(hu, pu_ref[...], cwu_ref[...], bu_ref[...], pos)
        act_ref[0:sample_rows, :] = (_gelu_tanh(g) * u).astype(act_ref.dtype)
        hsg_ref[...] = hg
        hsu_ref[...] = hu


def _ffn_up(x, w_up, layer, conv_w, conv_b, p_state, *, n_seq, seq_len, sample_rows, sample_len, bm, bn):
    t, d = x.shape
    f = w_up.shape[2] // 2
    tp = n_seq * seq_len
    assert seq_len % bm == 0 and f % bn == 0 and t == tp + sample_rows and sample_rows <= bm
    npt = tp // bm
    nj = f // bn
    kern = functools.partial(_ffn_up_kernel, tiles_per_seq=seq_len // bm, n_prompt_tiles=npt,
                             sample_rows=sample_rows, sample_len=sample_len)

    def half(rows, base):
        return pl.BlockSpec((rows, bn), lambda j, m: (0, base + j))

    def w_half(base):
        return pl.BlockSpec((None, d, bn), lambda j, m: (layer, 0, base + j))

    tail_spec = pl.BlockSpec((8, bn), lambda j, m: (jnp.minimum(m, npt - 1), j))
    return pl.pallas_call(
        kern, grid=(nj, npt + 1),
        in_specs=[pl.BlockSpec((bm, d), lambda j, m: (m, 0)),
                  w_half(0), w_half(nj), half(3, 0), half(3, nj), half(1, 0), half(1, nj),
                  half(sample_rows, 0), half(sample_rows, nj)],
        out_specs=[pl.BlockSpec((bm, bn), lambda j, m: (m, j)), tail_spec, tail_spec,
                   half(sample_rows, 0), half(sample_rows, 0)],
        out_shape=[jax.ShapeDtypeStruct((t, f), BF16),
                   jax.ShapeDtypeStruct((npt * 8, f), F32), jax.ShapeDtypeStruct((npt * 8, f), F32),
                   jax.ShapeDtypeStruct((sample_rows, f), F32), jax.ShapeDtypeStruct((sample_rows, f), F32)],
        scratch_shapes=[pltpu.VMEM((d, bn), BF16), pltpu.VMEM((d, bn), BF16),
                        pltpu.VMEM((bm, bn), F32), pltpu.VMEM((bm, bn), F32)],
        compiler_params=_params(2), name="ffn_up",
    )(x, w_up, w_up, conv_w, conv_w, conv_b, conv_b, p_state, p_state)


def _ple_kernel(x_ref, p_ref, wg_ref, wp_ref, h_ref, o_ref, wgb_ref, wpb_ref):
    @pl.when(pl.program_id(1) == 0)
    def _():
        wgb_ref[...] = wg_ref[...].astype(BF16)
        wpb_ref[...] = wp_ref[...].astype(BF16)

    gate = _sigmoid(_dot(x_ref[...], wgb_ref[...]))
    o_ref[...] = h_ref[...] + gate * _dot(p_ref[...], wpb_ref[...])


def _ple_update(hn, p, w_gate, w_ple, layer, h, *, bm, bn):
    t, d = h.shape
    kp = p.shape[1]
    assert t % bm == 0 and d % bn == 0
    blk = pl.BlockSpec((bm, bn), lambda j, i: (i, j))
    return pl.pallas_call(
        _ple_kernel, grid=(d // bn, t // bm),
        in_specs=[pl.BlockSpec((bm, d), lambda j, i: (i, 0)), pl.BlockSpec((bm, kp), lambda j, i: (i, 0)),
                  pl.BlockSpec((None, d, bn), lambda j, i: (layer, 0, j)),
                  pl.BlockSpec((None, kp, bn), lambda j, i: (layer, 0, j)), blk],
        out_specs=blk, out_shape=jax.ShapeDtypeStruct((t, d), F32),
        scratch_shapes=[pltpu.VMEM((d, bn), BF16), pltpu.VMEM((kp, bn), BF16)],
        input_output_aliases={4: 0}, compiler_params=_params(2), name="ple_update",
    )(hn, p, w_gate, w_ple, h)


def _l2norm(x):
    return x * lax.rsqrt(jnp.sum(x * x, axis=-1, keepdims=True) + EPS)


def _causal_dwconv(x, buf, w):
    width = w.shape[0]
    L = x.shape[1]
    xp = jnp.concatenate([buf.astype(x.dtype), x], axis=1)
    y = xp[:, :L] * w[0]
    for j in range(1, width):
        y = y + xp[:, j:j + L] * w[j]
    return y, xp[:, L:]


def _gated_delta_rule(q, k, v, g, beta, s0):
    n, L, H, dk = q.shape
    dv = v.shape[-1]
    c = math.gcd(L, DELTA_CHUNK)
    nc = L // c

    def chunks(t):
        t = t.reshape((n, nc, c, H) + t.shape[3:])
        return jnp.moveaxis(t, (1, 3), (0, 2))

    qc, kc, vc, bc = chunks(q), chunks(k), chunks(v), chunks(beta)
    gc = jnp.cumsum(chunks(g), axis=-1)
    tri = jnp.tril(jnp.ones((c, c), bool))
    tri_strict = jnp.tril(jnp.ones((c, c), bool), -1)
    diff = gc[..., :, None] - gc[..., None, :]
    decay = jnp.where(tri, jnp.exp(jnp.where(tri, diff, 0.0)), 0.0)
    kk = jnp.einsum('znhrd,znhsd->znhrs', kc, kc)
    a_mat = jnp.where(tri_strict, bc[..., None] * kk * decay, 0.0)
    eye = jnp.eye(c, dtype=F32)
    rhs = jnp.concatenate([bc[..., None] * vc, (bc * jnp.exp(gc))[..., None] * kc], axis=-1)
    sol = lax.linalg.triangular_solve(eye + a_mat, rhs, left_side=True, lower=True, unit_diagonal=True)
    u_base, w_mat = sol[..., :dv], sol[..., dv:]
    qk = jnp.where(tri, jnp.einsum('znhrd,znhsd->znhrs', qc, kc) * decay, 0.0)
    q_dec = qc * jnp.exp(gc)[..., None]
    k_dec = kc * jnp.exp(gc[..., -1:] - gc)[..., None]
    g_last = jnp.exp(gc[..., -1])

    def step(s, xs):
        u_b, w_m, qk_m, q_d, k_d, g_l = xs
        u = u_b - jnp.einsum('nhrk,nhkv->nhrv', w_m, s)
        o = jnp.einsum('nhrk,nhkv->nhrv', q_d, s) + jnp.einsum('nhrs,nhsv->nhrv', qk_m, u)
        s = g_l[..., None, None] * s + jnp.einsum('nhsk,nhsv->nhkv', k_d, u)
        return s, o

    s_final, o = lax.scan(step, s0, (u_base, w_mat, qk, q_dec, k_dec, g_last))
    o = jnp.moveaxis(o, (0, 2), (1, 3)).reshape(n, L, H, dv)
    return o, s_final


def _delta_mixer(qkv, z, b_raw, a_raw, conv_buf, s0, conv_w, a_log, dt_bias, out_g):
    n, L, _ = qkv.shape
    y, conv_buf_new = _causal_dwconv(qkv, conv_buf, conv_w)
    y = jax.nn.silu(y.astype(F32))
    q, k, v = jnp.split(y, [A_QK, 2 * A_QK], axis=-1)
    q = _l2norm(q.reshape(n, L, A_HEADS, A_DK)) * (A_DK ** -0.5)
    k = _l2norm(k.reshape(n, L, A_HEADS, A_DK))
    v = v.reshape(n, L, A_HEADS, A_DV)
    beta = jax.nn.sigmoid(b_raw.astype(F32))
    g = -jnp.exp(a_log.astype(F32)) * jax.nn.softplus(a_raw.astype(F32) + dt_bias.astype(F32))
    o, s_new = _gated_delta_rule(q, k, v, g, beta, s0.astype(F32))
    o = o * lax.rsqrt(jnp.mean(o * o, axis=-1, keepdims=True) + EPS) * out_g.astype(F32)
    o = o * jax.nn.silu(z.astype(F32)).reshape(n, L, A_HEADS, A_DV)
    return o.reshape(n, L, A_VAL), conv_buf_new, s_new


def _sink_softmax(s, mask, sinks):
    s = jnp.where(mask, s, -jnp.inf)
    sk = sinks.astype(F32).reshape(B_KV_HEADS, B_GROUP, 1, 1)
    m = jnp.maximum(jnp.max(s, axis=-1, keepdims=True), sk)
    e = jnp.exp(s - m)
    return e / (jnp.sum(e, axis=-1, keepdims=True) + jnp.exp(sk - m))


def _window_attn_sample(q, k, v, k_buf, v_buf, sinks):
    n, L = q.shape[:2]
    wb = k_buf.shape[1]
    kk = jnp.concatenate([k_buf.astype(k.dtype), k], axis=1)
    vv = jnp.concatenate([v_buf.astype(v.dtype), v], axis=1)
    rel = (wb + jnp.arange(L))[:, None] - jnp.arange(wb + L)[None, :]
    mask = (rel >= 0) & (rel < WINDOW)
    s = jnp.einsum('nqkgd,nskd->nkgqs', q, kk).astype(F32) * (B_HD ** -0.5)
    pr = _sink_softmax(s, mask, sinks)
    o = jnp.einsum('nkgqs,nskd->nqkgd', pr.astype(vv.dtype), vv).reshape(n, L, B_Q)
    return o, kk[:, L:], vv[:, L:]


def _chunk_mlp(uv, ln_g, ln_b, w_s, b_s):
    n, L, _ = uv.shape
    u, v = jnp.split(jax.nn.gelu(uv, approximate=True), 2, axis=-1)
    vf = v.astype(F32)
    mu = jnp.mean(vf, axis=-1, keepdims=True)
    var = jnp.mean(jnp.square(vf - mu), axis=-1, keepdims=True)
    vn = (vf - mu) * lax.rsqrt(var + EPS) * ln_g.astype(F32) + ln_b.astype(F32)
    nc = -(-L // C_CHUNK)
    vp = jnp.pad(vn, ((0, 0), (0, nc * C_CHUNK - L), (0, 0))).reshape(n, nc, C_CHUNK, C_GROUPS, C_GC)
    w_causal = jnp.where(jnp.tril(jnp.ones((C_CHUNK, C_CHUNK), bool)), w_s, 0.0)
    mixed = jnp.einsum('gts,ncsgd->nctgd', w_causal, vp) + b_s.T[:, :, None]
    mixed = mixed.reshape(n, nc * C_CHUNK, C_HALF)[:, :L]
    return u * mixed, vn


def kernel(x_prompt, x_sample, p_prompt, p_sample, state_a_conv, state_delta, cache_win_k, cache_win_v, state_ffn_conv, norm_mix_pre, norm_mix_post, norm_ffn_pre, norm_ffn_post, norm_ple, w_in, a_conv_w, a_log, a_dt_bias, a_out_norm, b_sinks, c_ln_g, c_ln_b, c_w_s, c_b_s, w_br_a, w_br_b, w_br_c, w_o, w_up, ffn_conv_w, ffn_conv_b, w_down, w_ple, w_ple_gate):
    nb, ls = x_prompt.shape[:2]
    ns, lq = x_sample.shape[:2]
    tp, ts = nb * ls, ns * lq
    ple = p_prompt.shape[-1]
    h = jnp.concatenate([x_prompt.reshape(tp, D_MODEL), x_sample.reshape(ts, D_MODEL)], axis=0)
    shifted = functools.partial(_matmul_shifted, bm=BM, bn=BN, shift=SHIFT)
    st_p, st_s = [], []
    for i in range(DEPTH):
        xn = _rms_norm_bf16(h, norm_mix_pre[i])
        proj_a = _matmul(xn, w_in, i, bm=BM, bn=BN, n_cols=OFF_BA)
        proj_ba = _matmul(xn, w_in, i, bm=BM, bn=LANES, col0=OFF_BA, n_cols=LANES)
        proj_q = shifted(xn, w_in, i, col0=OFF_BQ, n_cols=B_Q)
        proj_kv = shifted(xn, w_in, i, col0=OFF_KV, n_cols=2 * B_KV)
        proj_uv = shifted(xn, w_in, i, col0=OFF_CUV, n_cols=2 * C_HALF)
        proj_g = shifted(xn, w_in, i, col0=OFF_GATES, n_cols=N_BRANCH * D_MODEL)
        alog_row = jnp.zeros((1, LANES), F32).at[0, A_HEADS:2 * A_HEADS].set(a_log[i])
        dtb_row = jnp.zeros((1, LANES), F32).at[0, A_HEADS:2 * A_HEADS].set(a_dt_bias[i])
        ya_p, s_p = _delta_prompt(proj_a, proj_ba, a_conv_w[i], alog_row, dtb_row, a_out_norm[i].reshape(1, A_DV),
                                  n_seq=nb, seq_len=ls, n_heads=A_HEADS, chunk=math.gcd(ls, DELTA_CHUNK))
        abuf_p = proj_a[:tp, :A_CONV_CH].reshape(nb, ls, A_CONV_CH)[:, ls - (A_CONV - 1):]
        pa_s, pba_s = proj_a[tp:], proj_ba[tp:]
        ya_s, abuf_s, s_s = _delta_mixer(
            pa_s[:, :A_CONV_CH].reshape(ns, lq, A_CONV_CH), pa_s[:, A_CONV_CH:].reshape(ns, lq, A_VAL),
            pba_s[:, :A_HEADS].reshape(ns, lq, A_HEADS), pba_s[:, A_HEADS:2 * A_HEADS].reshape(ns, lq, A_HEADS),
            state_a_conv[i], state_delta[i], a_conv_w[i], a_log[i], a_dt_bias[i], a_out_norm[i])
        yb_p = _attn_prompt(proj_q, proj_kv, b_sinks[i], n_seq=nb, seq_len=ls)
        wb = min(WINDOW, ls)
        kv_tail = proj_kv[:tp].reshape(nb, ls, 2, B_KV_HEADS, B_HD)[:, ls - wb:]
        k_p, v_p = kv_tail[:, :, 0], kv_tail[:, :, 1]
        kv_s = proj_kv[tp:].reshape(ns, lq, 2, B_KV_HEADS, B_HD)
        yb_s, k_s, v_s = _window_attn_sample(proj_q[tp:].reshape(ns, lq, B_KV_HEADS, B_GROUP, B_HD),
                                             kv_s[:, :, 0], kv_s[:, :, 1], cache_win_k[i], cache_win_v[i], b_sinks[i])
        yc_p = _cmlp_prompt(proj_uv, c_ln_g[i].reshape(1, C_HALF), c_ln_b[i].reshape(1, C_HALF), c_w_s[i],
                            c_b_s[i].T, n_rows=tp)
        yc_s, cv_s = _chunk_mlp(proj_uv[tp:].reshape(ns, lq, 2 * C_HALF), c_ln_g[i], c_ln_b[i], c_w_s[i], c_b_s[i])
        y_a = jnp.concatenate([ya_p, ya_s.reshape(ts, A_VAL).astype(BF16)])
        y_b = jnp.concatenate([yb_p, yb_s.reshape(ts, B_Q).astype(BF16)])
        y_c = jnp.concatenate([yc_p, yc_s.reshape(ts, C_HALF).astype(BF16)])
        merged = _gated_merge(y_a, y_b, y_c, w_br_a, w_br_b, w_br_c, i, proj_g, bm=BM, bn=BN)
        h, xf = _residual_norm(h, _matmul(merged, w_o, i, bm=BM, bn=BN), norm_mix_post[i], norm_ffn_pre[i])
        p_state = jnp.roll(jnp.pad(state_ffn_conv[i], ((0, 0), (lq - (FFN_CONV - 1), 0), (0, 0))), -1, axis=0)
        act, tail_g, tail_u, hs_g, hs_u = _ffn_up(
            xf, w_up, i, ffn_conv_w[i], ffn_conv_b[i].reshape(1, 2 * D_FF), p_state.reshape(ts, 2 * D_FF),
            n_seq=nb, seq_len=ls, sample_rows=ts, sample_len=lq, bm=FFN_BM, bn=FFN_BN)
        tails = jnp.concatenate([tail_g, tail_u], axis=1).reshape(nb, ls // FFN_BM, 8, 2 * D_FF)
        fb_p = tails[:, -1, 8 - (FFN_CONV - 1):]
        fb_s = jnp.concatenate([hs_g, hs_u], axis=1).reshape(ns, lq, 2 * D_FF)[:, lq - (FFN_CONV - 1):]
        f = None
        for c in range(D_FF // D_MODEL):
            f = _matmul(act, w_down, i, bm=BM, bn=BN, k_chunk=c, bk=D_MODEL, acc=f)
        h, hn = _residual_norm(h, f, norm_ffn_post[i], norm_ple[i])
        p_i = jnp.concatenate([p_prompt[i].reshape(tp, ple), p_sample[i].reshape(ts, ple)]).astype(BF16)
        h = _ple_update(hn, p_i, w_ple_gate, w_ple, i, h, bm=BM, bn=BN)
        st_p.append((abuf_p, s_p, k_p, v_p, fb_p))
        st_s.append((abuf_s, s_s, k_s, v_s, fb_s, cv_s))

    def stack(states, j):
        return jnp.stack([s[j] for s in states])

    y_prompt = h[:tp].reshape(nb, ls, D_MODEL)
    y_sample = h[tp:].reshape(ns, lq, D_MODEL)
    return (y_prompt, y_sample,
            stack(st_p, 0), stack(st_p, 1), stack(st_p, 2), stack(st_p, 3), stack(st_p, 4),
            stack(st_s, 0), stack(st_s, 1), stack(st_s, 2), stack(st_s, 3), stack(st_s, 4),
            stack(st_s, 5))
```

```python
import functools
import math

import jax
import jax.numpy as jnp
from jax import lax
from jax.experimental import pallas as pl
from jax.experimental.pallas import tpu as pltpu

D_MODEL = 4096
DEPTH = 4
A_HEADS = 8
A_DK = 128
A_DV = 128
A_QK = A_HEADS * A_DK
A_VAL = A_HEADS * A_DV
A_CONV = 4
A_CONV_CH = 2 * A_QK + A_VAL
DELTA_CHUNK = 64
B_HEADS = 16
B_KV_HEADS = 4
B_GROUP = B_HEADS // B_KV_HEADS
B_HD = 64
B_Q = B_HEADS * B_HD
B_KV = B_KV_HEADS * B_HD
WINDOW = 128
C_GROUPS = 8
C_GC = 128
C_HALF = C_GROUPS * C_GC
C_CHUNK = 128
D_FF = 3 * D_MODEL
FFN_CONV = 3
N_BRANCH = 3
EPS = 1e-6

F32 = jnp.float32
BF16 = jnp.bfloat16
LANES = 128

VMEM_LIMIT_BYTES = 56 * 1024 * 1024

DELTA_RT = 256
DELTA_HP = 4
NEUMANN_BLOCK = 16
NORM_ROWS = 264

BM = 1056
BN = 512
FFN_BM = 1024
FFN_BN = 256
OFF_BA = A_CONV_CH + A_VAL
OFF_REST = OFF_BA + 2 * A_HEADS
OFF_KV = B_Q
OFF_CUV = OFF_KV + 2 * B_KV
OFF_GATES = OFF_CUV + 2 * C_HALF
BN_WIDE = 1024


def _params(n_axes):
    return pltpu.CompilerParams(dimension_semantics=("arbitrary",) * n_axes, vmem_limit_bytes=VMEM_LIMIT_BYTES)


def _sigmoid(x):
    return 1.0 / (1.0 + jnp.exp(-x))


def _softplus(x):
    return jnp.maximum(x, 0.0) + jnp.log(1.0 + jnp.exp(-jnp.abs(x)))


def _gelu_tanh(x):
    return 0.5 * x * (1.0 + jnp.tanh(0.7978845608028654 * (x + 0.044715 * (x * x * x))))


def _rms(x, g):
    return x * lax.rsqrt(jnp.mean(x * x, axis=-1, keepdims=True) + EPS) * g


def _dot(a, b):
    return jnp.dot(a, b, preferred_element_type=F32)


def _dot_nt(a, b):
    return lax.dot_general(a, b, (((1,), (1,)), ((), ())), preferred_element_type=F32)


def _dot_tn(a, b):
    return lax.dot_general(a, b, (((0,), (0,)), ((), ())), preferred_element_type=F32)


def _split2(x):
    hi = x.astype(BF16)
    lo = (x - hi.astype(F32)).astype(BF16)
    return hi, lo


def _dot3s(a, b):
    ah, al = a
    bh, bl = b
    return _dot(ah, bh) + (_dot(ah, bl) + _dot(al, bh))


def _mm_kernel(x_ref, w_ref, o_ref, wbf_ref):
    @pl.when(pl.program_id(1) == 0)
    def _():
        wbf_ref[...] = w_ref[...].astype(BF16)

    o_ref[...] = _dot(x_ref[...], wbf_ref[...])


def _mm_acc_kernel(x_ref, w_ref, a_ref, o_ref, wbf_ref):
    @pl.when(pl.program_id(1) == 0)
    def _():
        wbf_ref[...] = w_ref[...].astype(BF16)

    o_ref[...] = a_ref[...] + _dot(x_ref[...], wbf_ref[...])


def _mm_bf16_kernel(x_ref, w_ref, o_ref):
    o_ref[...] = _dot(x_ref[...], w_ref[...])


def _matmul(x, w, layer, *, bm, bn, n_cols=None, col0=0, k_chunk=0, bk=None, acc=None):
    m = x.shape[0]
    bk = w.shape[1] if bk is None else bk
    n_cols = w.shape[2] if n_cols is None else n_cols
    assert m % bm == 0 and n_cols % bn == 0 and col0 % bn == 0
    cb0 = col0 // bn
    grid = (n_cols // bn, m // bm)
    x_spec = pl.BlockSpec((bm, bk), lambda j, i: (i, k_chunk))
    w_spec = pl.BlockSpec((None, bk, bn), lambda j, i: (layer, k_chunk, cb0 + j))
    o_spec = pl.BlockSpec((bm, bn), lambda j, i: (i, j))
    out_shape = jax.ShapeDtypeStruct((m, n_cols), F32)
    if w.dtype == BF16:
        assert acc is None
        return pl.pallas_call(_mm_bf16_kernel, grid=grid, in_specs=[x_spec, w_spec], out_specs=o_spec,
                              out_shape=out_shape, compiler_params=_params(2), name="proj_bf16")(x, w)
    scratch = [pltpu.VMEM((bk, bn), BF16)]
    if acc is None:
        return pl.pallas_call(_mm_kernel, grid=grid, in_specs=[x_spec, w_spec], out_specs=o_spec,
                              out_shape=out_shape, scratch_shapes=scratch, compiler_params=_params(2),
                              name="proj")(x, w)
    return pl.pallas_call(_mm_acc_kernel, grid=grid, in_specs=[x_spec, w_spec, o_spec], out_specs=o_spec,
                          out_shape=out_shape, scratch_shapes=scratch, compiler_params=_params(2),
                          input_output_aliases={2: 0}, name="proj_acc")(x, w, acc)


def _norm_kernel(h_ref, g_ref, xn_ref):
    xn_ref[...] = _rms(h_ref[...], g_ref[...]).astype(xn_ref.dtype)


def _resnorm_kernel(h_ref, y_ref, gp_ref, gn_ref, ho_ref, xn_ref):
    h = h_ref[...] + _rms(y_ref[...], gp_ref[...])
    ho_ref[...] = h
    xn_ref[...] = _rms(h, gn_ref[...]).astype(xn_ref.dtype)


def _rms_norm_bf16(h, g):
    t, d = h.shape
    assert t % NORM_ROWS == 0
    row = pl.BlockSpec((NORM_ROWS, d), lambda r: (r, 0))
    vec = pl.BlockSpec((1, d), lambda r: (0, 0))
    return pl.pallas_call(_norm_kernel, grid=(t // NORM_ROWS,), in_specs=[row, vec], out_specs=row,
                          out_shape=jax.ShapeDtypeStruct((t, d), BF16), compiler_params=_params(1),
                          name="rms_norm")(h, g.reshape(1, d))


def _residual_norm(h, y, g_post, g_next):
    t, d = h.shape
    assert t % NORM_ROWS == 0
    row = pl.BlockSpec((NORM_ROWS, d), lambda r: (r, 0))
    vec = pl.BlockSpec((1, d), lambda r: (0, 0))
    return pl.pallas_call(
        _resnorm_kernel, grid=(t // NORM_ROWS,), in_specs=[row, row, vec, vec], out_specs=[row, row],
        out_shape=[jax.ShapeDtypeStruct((t, d), F32), jax.ShapeDtypeStruct((t, d), BF16)],
        input_output_aliases={0: 0}, compiler_params=_params(1), name="residual_norm",
    )(h, y, g_post.reshape(1, d), g_next.reshape(1, d))


def _unit_lower_inverse(a, rx, c):
    eye = (rx == 0).astype(F32)
    blk = min(NEUMANN_BLOCK, c)
    n = -jnp.where(rx < blk, a, 0.0)
    p = eye + n
    pw = n
    for _ in range(int(math.log2(blk)) - 1):
        pws = _split2(pw)
        pw = _dot3s(pws, pws)
        p = p + _dot3s(_split2(p), _split2(pw))
    while blk < c:
        e = jnp.where((rx >= blk) & (rx < 2 * blk), a, 0.0)
        ps = _split2(p)
        p = p - _dot3s(ps, _split2(_dot3s(_split2(e), ps)))
        blk *= 2
    return p


def _shift_rows(x, prev, j, pos):
    rp = pltpu.roll(prev, j, axis=0)
    if prev.shape[0] != x.shape[0]:
        rp = jnp.broadcast_to(rp[None], (x.shape[0] // 8, 8, x.shape[1])).reshape(x.shape)
    return jnp.where(pos < j, rp, pltpu.roll(x, j, axis=0))


def _conv_silu(x, xprev, cw, seg):
    pos = lax.broadcasted_iota(jnp.int32, x.shape, 0) & (seg - 1)
    acc = x * cw[3:4, :]
    for j in range(1, 4):
        acc = acc + _shift_rows(x, xprev, j, pos) * cw[3 - j:4 - j, :]
    return acc * _sigmoid(acc)


def _delta_tile_prep(q_raw, k_raw, v_raw, pq, pk, pv, cwq, cwk, cwv, beta_col, g_col, c, seg):
    rt = q_raw.shape[0]
    q = _conv_silu(q_raw, pq, cwq, seg)
    k = _conv_silu(k_raw, pk, cwk, seg)
    v = _conv_silu(v_raw, pv, cwv, seg)
    q = q * (lax.rsqrt(jnp.sum(q * q, axis=-1, keepdims=True) + EPS) * (A_DK ** -0.5))
    k = k * lax.rsqrt(jnp.sum(k * k, axis=-1, keepdims=True) + EPS)

    r_i = lax.broadcasted_iota(jnp.int32, (rt, rt), 0)
    c_i = lax.broadcasted_iota(jnp.int32, (rt, rt), 1)
    rx = r_i ^ c_i
    same = rx < c
    lower = same & (c_i <= r_i)
    strict = same & (c_i < r_i)

    g1 = g_col.astype(BF16)
    r1 = g_col - g1.astype(F32)
    g2 = r1.astype(BF16)
    g3 = (r1 - g2.astype(F32)).astype(BF16)
    lmat = lower.astype(BF16)
    umat = (same & (r_i <= c_i)).astype(BF16)
    ones = jnp.ones((rt, rt), BF16)
    gc = None
    gr = None
    for gp in (g1, g2, g3):
        a = _dot(lmat, jnp.broadcast_to(gp, (rt, LANES)))
        b = _dot(ones, jnp.broadcast_to(gp, (rt, rt)) * umat)
        gc = a if gc is None else gc + a
        gr = b if gr is None else gr + b
    gc_full = jnp.concatenate([gc] * (rt // LANES), axis=1)
    decay = jnp.where(lower, jnp.exp(jnp.where(lower, gc_full - gr, 0.0)), 0.0)

    kb = k.astype(BF16)
    kk = _dot_nt(kb, kb)
    a_mat = jnp.where(strict, beta_col * kk * decay, 0.0)
    t_inv = _unit_lower_inverse(a_mat, rx, c)
    rhs = jnp.concatenate([beta_col * v, (beta_col * jnp.exp(gc)) * k], axis=1)
    sol = _dot3s(_split2(t_inv), _split2(rhs))
    u_base = sol[:, :LANES]
    w_mat = sol[:, LANES:]
    qk = jnp.where(lower, _dot_nt(q.astype(BF16), kb) * decay, 0.0)
    q_dec = q * jnp.exp(gc)
    return k, gc, u_base, w_mat, qk, q_dec


def _gated_out_norm(o, z, og):
    return _rms(o, og) * (z * _sigmoid(z))


def _head_cols(ba, alog_row, dtb_row, h, n_heads):
    lane = lax.broadcasted_iota(jnp.int32, ba.shape, 1)
    sig = _sigmoid(ba)
    g_all = -jnp.exp(alog_row) * _softplus(ba + dtb_row)
    beta_col = jnp.sum(jnp.where(lane == h, sig, 0.0), axis=1, keepdims=True)
    g_col = jnp.sum(jnp.where(lane == n_heads + h, g_all, 0.0), axis=1, keepdims=True)
    return beta_col, g_col


def _delta_prompt_kernel(q_ref, k_ref, v_ref, z_ref, ba_ref, cwq_ref, cwk_ref, cwv_ref, alog_ref, dtb_ref, og_ref,
                         y_ref, sfin_ref, xprev_ref, s_ref, *, n_heads, chunk):
    j = pl.program_id(2)
    hp = pl.program_id(1)
    rt = q_ref.shape[0]

    @pl.when(j == 0)
    def _():
        xprev_ref[...] = jnp.zeros_like(xprev_ref)
        s_ref[...] = jnp.zeros_like(s_ref)

    ba = ba_ref[...]
    for hh in range(DELTA_HP):
        cs = slice(hh * LANES, (hh + 1) * LANES)
        beta_col, g_col = _head_cols(ba, alog_ref[...], dtb_ref[...], hp * DELTA_HP + hh, n_heads)
        q_raw, k_raw, v_raw = q_ref[:, cs], k_ref[:, cs], v_ref[:, cs]
        k, gc, u_base, w_mat, qk, q_dec = _delta_tile_prep(
            q_raw, k_raw, v_raw, xprev_ref[0, :, cs], xprev_ref[1, :, cs], xprev_ref[2, :, cs],
            cwq_ref[:, cs], cwk_ref[:, cs], cwv_ref[:, cs], beta_col, g_col, chunk, rt)
        xprev_ref[0, :, cs] = q_raw[rt - 8:]
        xprev_ref[1, :, cs] = k_raw[rt - 8:]
        xprev_ref[2, :, cs] = v_raw[rt - 8:]
        s = s_ref[hh]
        us, os_ = [], []
        for cc in range(rt // chunk):
            rs = slice(cc * chunk, (cc + 1) * chunk)
            sb = s.astype(BF16)
            u_c = u_base[rs] - _dot(w_mat[rs].astype(BF16), sb)
            os_.append(_dot(q_dec[rs].astype(BF16), sb))
            gl = gc[(cc + 1) * chunk - 1:(cc + 1) * chunk, :]
            k_dec = k[rs] * jnp.exp(gl - gc[rs])
            s = jnp.exp(gl) * s + _dot_tn(k_dec.astype(BF16), u_c.astype(BF16))
            us.append(u_c)
        s_ref[hh] = s
        u_all = jnp.concatenate(us, axis=0)
        o = jnp.concatenate(os_, axis=0) + _dot(qk.astype(BF16), u_all.astype(BF16))
        y_ref[:, cs] = _gated_out_norm(o, z_ref[:, cs], og_ref[...]).astype(y_ref.dtype)

    @pl.when(j == pl.num_programs(2) - 1)
    def _():
        sfin_ref[0] = s_ref[...]


def _delta_prompt(proj_a, proj_ba, conv_w, alog_row, dtb_row, out_g, *, n_seq, seq_len, n_heads, chunk):
    rt, hp = DELTA_RT, DELTA_HP
    assert seq_len % rt == 0 and rt % chunk == 0 and n_heads % hp == 0
    tiles = seq_len // rt
    nhp = n_heads // hp
    wblk = hp * LANES

    def col_spec(base):
        return pl.BlockSpec((rt, wblk), lambda n, h, j: (n * tiles + j, base + h))

    def cw_spec(base):
        return pl.BlockSpec((4, wblk), lambda n, h, j: (0, base + h))

    row128 = pl.BlockSpec((1, LANES), lambda n, h, j: (0, 0))
    kern = functools.partial(_delta_prompt_kernel, n_heads=n_heads, chunk=chunk)
    return pl.pallas_call(
        kern, grid=(n_seq, nhp, tiles),
        in_specs=[col_spec(0), col_spec(nhp), col_spec(2 * nhp), col_spec(3 * nhp),
                  pl.BlockSpec((rt, LANES), lambda n, h, j: (n * tiles + j, 0)),
                  cw_spec(0), cw_spec(nhp), cw_spec(2 * nhp), row128, row128, row128],
        out_specs=[pl.BlockSpec((rt, wblk), lambda n, h, j: (n * tiles + j, h)),
                   pl.BlockSpec((1, hp, LANES, LANES), lambda n, h, j: (n, h, 0, 0))],
        out_shape=[jax.ShapeDtypeStruct((n_seq * seq_len, n_heads * LANES), BF16),
                   jax.ShapeDtypeStruct((n_seq, n_heads, LANES, LANES), F32)],
        scratch_shapes=[pltpu.VMEM((3, 8, wblk), F32), pltpu.VMEM((hp, LANES, LANES), F32)],
        compiler_params=_params(3), name="delta_prompt",
    )(proj_a, proj_a, proj_a, proj_a, proj_ba, conv_w, conv_w, conv_w, alog_row, dtb_row, out_g)


def _attn_prompt_kernel(q_ref, kc_ref, kp_ref, vc_ref, vp_ref, sk_ref, o_ref, *, blocks_per_seq):
    b = pl.program_id(0) % blocks_per_seq
    w = q_ref.shape[0]
    r_i = lax.broadcasted_iota(jnp.int32, (w, 2 * w), 0)
    c_i = lax.broadcasted_iota(jnp.int32, (w, 2 * w), 1)
    mask = (c_i > r_i) & (c_i <= r_i + w) & ((b > 0) | (c_i >= w))
    scale = B_HD ** -0.5
    for kvh in range(B_KV_HEADS):
        ks = slice(kvh * B_HD, (kvh + 1) * B_HD)
        kk = jnp.concatenate([kp_ref[:, ks], kc_ref[:, ks]], axis=0).astype(BF16)
        vv = jnp.concatenate([vp_ref[:, ks], vc_ref[:, ks]], axis=0).astype(BF16)
        for g in range(B_GROUP):
            h = kvh * B_GROUP + g
            hs = slice(h * B_HD, (h + 1) * B_HD)
            s = _dot_nt(q_ref[:, hs].astype(BF16), kk) * scale
            s = jnp.where(mask, s, -jnp.inf)
            sink = sk_ref[h]
            m = jnp.maximum(jnp.max(s, axis=-1, keepdims=True), sink)
            e = jnp.exp(s - m)
            p = e / (jnp.sum(e, axis=-1, keepdims=True) + jnp.exp(sink - m))
            o_ref[:, hs] = _dot(p.astype(BF16), vv).astype(o_ref.dtype)


def _attn_prompt(q_arr, kv_arr, sinks, *, n_seq, seq_len):
    w = WINDOW
    bps = seq_len // w
    cur = lambda r: (r, 0)
    return pl.pallas_call(
        functools.partial(_attn_prompt_kernel, blocks_per_seq=bps), grid=(n_seq * bps,),
        in_specs=[pl.BlockSpec((w, B_Q), cur),
                  pl.BlockSpec((w, B_KV), cur), pl.BlockSpec((w, B_KV), lambda r: (jnp.maximum(r - 1, 0), 0)),
                  pl.BlockSpec((w, B_KV), lambda r: (r, 1)), pl.BlockSpec((w, B_KV), lambda r: (jnp.maximum(r - 1, 0), 1)),
                  pl.BlockSpec(memory_space=pltpu.SMEM)],
        out_specs=pl.BlockSpec((w, B_Q), cur),
        out_shape=jax.ShapeDtypeStruct((n_seq * seq_len, B_Q), BF16),
        compiler_params=_params(1), name="attn_prompt",
    )(q_arr, kv_arr, kv_arr, kv_arr, kv_arr, sinks)


def _cmlp_kernel(u_ref, v_ref, lg_ref, lb_ref, ws_ref, bs_ref, o_ref, *vn_ref):
    v = _gelu_tanh(v_ref[...])
    mu = jnp.mean(v, axis=-1, keepdims=True)
    d = v - mu
    var = jnp.mean(d * d, axis=-1, keepdims=True)
    vn = d * lax.rsqrt(var + EPS) * lg_ref[...] + lb_ref[...]
    if vn_ref:
        vn_ref[0][...] = vn
    c = v.shape[0]
    tril = lax.broadcasted_iota(jnp.int32, (c, c), 1) <= lax.broadcasted_iota(jnp.int32, (c, c), 0)
    for g in range(C_GROUPS):
        gs = slice(g * C_GC, (g + 1) * C_GC)
        wc = jnp.where(tril, ws_ref[g], 0.0).astype(BF16)
        mixed = _dot(wc, vn[:, gs].astype(BF16)) + bs_ref[:, g:g + 1]
        o_ref[:, gs] = (_gelu_tanh(u_ref[:, gs]) * mixed).astype(o_ref.dtype)


def _cmlp(uv_arr, ln_g, ln_b, w_s, b_s_t, *, row0, n_rows, chunk, want_vn):
    assert row0 % chunk == 0 and n_rows % chunk == 0
    rb0 = row0 // chunk
    blk = pl.BlockSpec((chunk, C_HALF), lambda r: (r, 0))
    vec = pl.BlockSpec((1, C_HALF), lambda r: (0, 0))
    out_specs = [blk, blk] if want_vn else blk
    y_shape = jax.ShapeDtypeStruct((n_rows, C_HALF), BF16)
    out_shape = [y_shape, jax.ShapeDtypeStruct((n_rows, C_HALF), F32)] if want_vn else y_shape
    return pl.pallas_call(
        _cmlp_kernel, grid=(n_rows // chunk,),
        in_specs=[pl.BlockSpec((chunk, C_HALF), lambda r: (rb0 + r, 0)),
                  pl.BlockSpec((chunk, C_HALF), lambda r: (rb0 + r, 1)), vec, vec,
                  pl.BlockSpec((C_GROUPS, chunk, chunk), lambda r: (0, 0, 0)),
                  pl.BlockSpec((chunk, C_GROUPS), lambda r: (0, 0))],
        out_specs=out_specs, out_shape=out_shape, compiler_params=_params(1), name="cmlp",
    )(uv_arr, uv_arr, ln_g, ln_b, w_s, b_s_t)


def _delta_sample_kernel(q_ref, k_ref, v_ref, z_ref, ba_ref, pq_ref, pk_ref, pv_ref, cwq_ref, cwk_ref, cwv_ref,
                         alog_ref, dtb_ref, og_ref, s0_ref, y_ref, snew_ref,
                         ub_ref, wm_ref, qd_ref, kd_ref, eg_ref, u_ref, oi_ref, *, n_heads, seq):
    h = pl.program_id(0)
    rt = q_ref.shape[0]
    beta_col, g_col = _head_cols(ba_ref[...], alog_ref[...], dtb_ref[...], h, n_heads)
    k, gc, u_base, w_mat, qk, q_dec = _delta_tile_prep(
        q_ref[...], k_ref[...], v_ref[...], pq_ref[...], pk_ref[...], pv_ref[...],
        cwq_ref[...], cwk_ref[...], cwv_ref[...], beta_col, g_col, seq, seq)
    r_i = lax.broadcasted_iota(jnp.int32, (rt, rt), 0)
    c_i = lax.broadcasted_iota(jnp.int32, (rt, rt), 1)
    sel = (c_i == (r_i | (seq - 1))).astype(BF16)
    g1 = gc.astype(BF16)
    r1 = gc - g1.astype(F32)
    g2 = r1.astype(BF16)
    g3 = (r1 - g2.astype(F32)).astype(BF16)
    gl = _dot(sel, g1) + (_dot(sel, g2) + _dot(sel, g3))
    pad = jnp.zeros((8, LANES), F32)
    for ref, val in ((ub_ref, u_base), (wm_ref, w_mat), (qd_ref, q_dec), (kd_ref, k * jnp.exp(gl - gc)),
                     (eg_ref, jnp.exp(gl))):
        ref[0:rt, :] = val
        ref[rt:rt + 8, :] = pad

    def body(s, carry):
        r0 = pl.multiple_of(s * seq, seq)
        win = pl.ds(r0, 2 * seq)
        s0 = s0_ref[s]
        sb = s0.astype(BF16)
        first = lax.broadcasted_iota(jnp.int32, (2 * seq, LANES), 0) < seq
        u16 = ub_ref[win, :] - _dot(wm_ref[win, :].astype(BF16), sb)
        o16 = _dot(qd_ref[win, :].astype(BF16), sb)
        kd16 = jnp.where(first, kd_ref[win, :], 0.0)
        snew_ref[s] = eg_ref[pl.ds(r0, 1), :] * s0 + _dot_tn(kd16.astype(BF16), jnp.where(first, u16, 0.0).astype(BF16))
        u_ref[pl.ds(r0, seq), :] = u16[0:seq]
        oi_ref[pl.ds(r0, seq), :] = o16[0:seq]
        return carry

    lax.fori_loop(0, rt // seq, body, 0)
    o = oi_ref[...] + _dot(qk.astype(BF16), u_ref[...].astype(BF16))
    y_ref[...] = _gated_out_norm(o, z_ref[...], og_ref[...]).astype(y_ref.dtype)


def _delta_sample(proj_a, proj_ba, p_conv, conv_w, alog_row, dtb_row, out_g, s0, *, row0, n_seq, seq, n_heads):
    rt = n_seq * seq
    assert seq == 8 and row0 % rt == 0 and rt % LANES == 0
    rb = row0 // rt

    def col_spec(base):
        return pl.BlockSpec((rt, LANES), lambda h: (rb, base + h))

    def p_spec(base):
        return pl.BlockSpec((rt, LANES), lambda h: (0, base + h))

    def cw_spec(base):
        return pl.BlockSpec((4, LANES), lambda h: (0, base + h))

    row128 = pl.BlockSpec((1, LANES), lambda h: (0, 0))
    st_spec = pl.BlockSpec((n_seq, None, LANES, LANES), lambda h: (0, h, 0, 0))
    big = pltpu.VMEM((rt + 8, LANES), F32)
    kern = functools.partial(_delta_sample_kernel, n_heads=n_heads, seq=seq)
    return pl.pallas_call(
        kern, grid=(n_heads,),
        in_specs=[col_spec(0), col_spec(n_heads), col_spec(2 * n_heads), col_spec(3 * n_heads),
                  pl.BlockSpec((rt, LANES), lambda h: (rb, 0)),
                  p_spec(0), p_spec(n_heads), p_spec(2 * n_heads),
                  cw_spec(0), cw_spec(n_heads), cw_spec(2 * n_heads), row128, row128, row128, st_spec],
        out_specs=[pl.BlockSpec((rt, LANES), lambda h: (0, h)), st_spec],
        out_shape=[jax.ShapeDtypeStruct((rt, n_heads * LANES), BF16),
                   jax.ShapeDtypeStruct((n_seq, n_heads, LANES, LANES), F32)],
        scratch_shapes=[big, big, big, big, big, pltpu.VMEM((rt, LANES), F32), pltpu.VMEM((rt, LANES), F32)],
        compiler_params=_params(1), name="delta_sample",
    )(proj_a, proj_a, proj_a, proj_a, proj_ba, p_conv, p_conv, p_conv, conv_w, conv_w, conv_w,
      alog_row, dtb_row, out_g, s0)


def _attn_sample_kernel(q_ref, kvn_ref, kb_ref, vb_ref, sk_ref, o_ref):
    lq = q_ref.shape[0]
    wb = kb_ref.shape[0]
    nk = 2 * wb
    rows = B_GROUP * lq
    r_i = lax.broadcasted_iota(jnp.int32, (rows, nk), 0)
    c_i = lax.broadcasted_iota(jnp.int32, (rows, nk), 1)
    i = r_i & (lq - 1)
    mask = (c_i <= i + wb) & (c_i > i + wb - WINDOW)
    grp = lax.shift_right_logical(lax.broadcasted_iota(jnp.int32, (rows, 1), 0), int(math.log2(lq)))
    scale = B_HD ** -0.5
    zpad = jnp.zeros((nk - wb - lq, B_HD), F32)
    for kvh in range(B_KV_HEADS):
        ks = slice(kvh * B_HD, (kvh + 1) * B_HD)
        vs = slice(B_KV + kvh * B_HD, B_KV + (kvh + 1) * B_HD)
        kk = jnp.concatenate([kb_ref[:, ks], kvn_ref[:, ks], zpad], axis=0).astype(BF16)
        vv = jnp.concatenate([vb_ref[:, ks], kvn_ref[:, vs], zpad], axis=0).astype(BF16)
        h0 = kvh * B_GROUP
        q = jnp.concatenate([q_ref[:, (h0 + g) * B_HD:(h0 + g + 1) * B_HD] for g in range(B_GROUP)], axis=0)
        s = jnp.where(mask, _dot_nt(q.astype(BF16), kk) * scale, -jnp.inf)
        sink = jnp.zeros((rows, 1), F32)
        for g in range(B_GROUP):
            sink = jnp.where(grp == g, sk_ref[h0 + g], sink)
        m = jnp.maximum(jnp.max(s, axis=-1, keepdims=True), sink)
        e = jnp.exp(s - m)
        p = e / (jnp.sum(e, axis=-1, keepdims=True) + jnp.exp(sink - m))
        o = _dot(p.astype(BF16), vv)
        for g in range(B_GROUP):
            o_ref[:, (h0 + g) * B_HD:(h0 + g + 1) * B_HD] = o[g * lq:(g + 1) * lq]


def _attn_sample(q_arr, kv_arr, k_buf, v_buf, sinks, *, row0, n_seq, seq):
    assert row0 % seq == 0 and seq & (seq - 1) == 0 and k_buf.shape[1] >= seq
    rb0 = row0 // seq
    wb = k_buf.shape[1]
    buf = pl.BlockSpec((None, wb, B_KV), lambda s: (s, 0, 0))
    return pl.pallas_call(
        _attn_sample_kernel, grid=(n_seq,),
        in_specs=[pl.BlockSpec((seq, B_Q), lambda s: (rb0 + s, 0)), pl.BlockSpec((seq, 2 * B_KV), lambda s: (rb0 + s, 0)),
                  buf, buf, pl.BlockSpec(memory_space=pltpu.SMEM)],
        out_specs=pl.BlockSpec((seq, B_Q), lambda s: (s, 0)),
        out_shape=jax.ShapeDtypeStruct((n_seq * seq, B_Q), F32),
        compiler_params=_params(1), name="attn_sample",
    )(q_arr, kv_arr, k_buf, v_buf, sinks)


def _merge_kernel(ya_ref, yb_ref, yc_ref, wa_ref, wb_ref, wc_ref, ga_ref, gb_ref, gc_ref, o_ref,
                  wab_ref, wbb_ref, wcb_ref):
    @pl.when(pl.program_id(1) == 0)
    def _():
        wab_ref[...] = wa_ref[...].astype(BF16)
        wbb_ref[...] = wb_ref[...].astype(BF16)
        wcb_ref[...] = wc_ref[...].astype(BF16)

    acc = _sigmoid(ga_ref[...]) * _dot(ya_ref[...], wab_ref[...])
    acc = acc + _sigmoid(gb_ref[...]) * _dot(yb_ref[...], wbb_ref[...])
    acc = acc + _sigmoid(gc_ref[...]) * _dot(yc_ref[...], wcb_ref[...])
    o_ref[...] = acc.astype(o_ref.dtype)


def _gated_merge(y_a, y_b, y_c, w_a, w_b, w_c, layer, gate_logits, *, bm, bn):
    t, kdim = y_a.shape
    d = w_a.shape[2]
    assert t % bm == 0 and d % bn == 0
    nj = d // bn
    y_spec = pl.BlockSpec((bm, kdim), lambda j, i: (i, 0))
    w_spec = pl.BlockSpec((None, kdim, bn), lambda j, i: (layer, 0, j))

    def g_spec(b):
        return pl.BlockSpec((bm, bn), lambda j, i: (i, b * nj + j))

    return pl.pallas_call(
        _merge_kernel, grid=(nj, t // bm),
        in_specs=[y_spec, y_spec, y_spec, w_spec, w_spec, w_spec, g_spec(0), g_spec(1), g_spec(2)],
        out_specs=pl.BlockSpec((bm, bn), lambda j, i: (i, j)),
        out_shape=jax.ShapeDtypeStruct((t, d), BF16),
        scratch_shapes=[pltpu.VMEM((kdim, bn), BF16)] * 3,
        compiler_params=_params(2), name="gated_merge",
    )(y_a, y_b, y_c, w_a, w_b, w_c, gate_logits, gate_logits, gate_logits)


def _conv3(h, hprev, cw, b, pos):
    acc = h * cw[2:3, :] + b
    for j in (1, 2):
        acc = acc + _shift_rows(h, hprev, j, pos) * cw[2 - j:3 - j, :]
    return acc


def _ffn_up_kernel(x_ref, wg_ref, wu_ref, cwg_ref, cwu_ref, bg_ref, bu_ref, pg_ref, pu_ref,
                   act_ref, tg_ref, tu_ref, hsg_ref, hsu_ref, wgb_ref, wub_ref, prevg_ref, prevu_ref,
                   *, tiles_per_seq, n_prompt_tiles, sample_rows, sample_len):
    m = pl.program_id(1)
    bm = x_ref.shape[0]

    @pl.when(m == 0)
    def _():
        wgb_ref[...] = wg_ref[...].astype(BF16)
        wub_ref[...] = wu_ref[...].astype(BF16)

    @pl.when(m < n_prompt_tiles)
    def _():
        x = x_ref[...]
        hg = _dot(x, wgb_ref[...])
        hu = _dot(x, wub_ref[...])
        first = (m % tiles_per_seq) == 0
        pos = lax.broadcasted_iota(jnp.int32, hg.shape, 0)
        pg = jnp.where(first, 0.0, prevg_ref[...])
        pu = jnp.where(first, 0.0, prevu_ref[...])
        g = _conv3(hg, pg, cwg_ref[...], bg_ref[...], pos)
        u = _conv3(hu, pu, cwu_ref[...], bu_ref[...], pos)
        act_ref[...] = (_gelu_tanh(g) * u).astype(act_ref.dtype)
        prevg_ref[...] = hg
        prevu_ref[...] = hu
        tg_ref[...] = hg[bm - 8:, :]
        tu_ref[...] = hu[bm - 8:, :]

    @pl.when(m == n_prompt_tiles)
    def _():
        x = x_ref[0:sample_rows, :]
        hg = _dot(x, wgb_ref[...])
        hu = _dot(x, wub_ref[...])
        pos = lax.broadcasted_iota(jnp.int32, hg.shape, 0) & (sample_len - 1)
        g = _conv3(hg, pg_ref[...], cwg_ref[...], bg_ref[...], pos)
        u = _conv3(hu, pu_ref[...], cwu_ref[...], bu_ref[...], pos)
        act_ref[0:sample_rows, :] = (_gelu_tanh(g) * u).astype(act_ref.dtype)
        hsg_ref[...] = hg
        hsu_ref[...] = hu


def _ffn_up(x, w_up, layer, conv_w, conv_b, p_state, *, n_seq, seq_len, sample_rows, sample_len, bm, bn):
    t, d = x.shape
    f = w_up.shape[2] // 2
    tp = n_seq * seq_len
    assert seq_len % bm == 0 and f % bn == 0 and t == tp + sample_rows and sample_rows <= bm
    npt = tp // bm
    nj = f // bn
    kern = functools.partial(_ffn_up_kernel, tiles_per_seq=seq_len // bm, n_prompt_tiles=npt,
                             sample_rows=sample_rows, sample_len=sample_len)

    def half(rows, base):
        return pl.BlockSpec((rows, bn), lambda j, m: (0, base + j))

    def w_half(base):
        return pl.BlockSpec((None, d, bn), lambda j, m: (layer, 0, base + j))

    tail_spec = pl.BlockSpec((8, bn), lambda j, m: (jnp.minimum(m, npt - 1), j))
    return pl.pallas_call(
        kern, grid=(nj, npt + 1),
        in_specs=[pl.BlockSpec((bm, d), lambda j, m: (m, 0)),
                  w_half(0), w_half(nj), half(3, 0), half(3, nj), half(1, 0), half(1, nj),
                  half(sample_rows, 0), half(sample_rows, nj)],
        out_specs=[pl.BlockSpec((bm, bn), lambda j, m: (m, j)), tail_spec, tail_spec,
                   half(sample_rows, 0), half(sample_rows, 0)],
        out_shape=[jax.ShapeDtypeStruct((t, f), BF16),
                   jax.ShapeDtypeStruct((npt * 8, f), F32), jax.ShapeDtypeStruct((npt * 8, f), F32),
                   jax.ShapeDtypeStruct((sample_rows, f), F32), jax.ShapeDtypeStruct((sample_rows, f), F32)],
        scratch_shapes=[pltpu.VMEM((d, bn), BF16), pltpu.VMEM((d, bn), BF16),
                        pltpu.VMEM((bm, bn), F32), pltpu.VMEM((bm, bn), F32)],
        compiler_params=_params(2), name="ffn_up",
    )(x, w_up, w_up, conv_w, conv_w, conv_b, conv_b, p_state, p_state)


def _ple_kernel(x_ref, p_ref, wg_ref, wp_ref, h_ref, o_ref, wgb_ref, wpb_ref):
    @pl.when(pl.program_id(1) == 0)
    def _():
        wgb_ref[...] = wg_ref[...].astype(BF16)
        wpb_ref[...] = wp_ref[...].astype(BF16)

    gate = _sigmoid(_dot(x_ref[...], wgb_ref[...]))
    o_ref[...] = h_ref[...] + gate * _dot(p_ref[...], wpb_ref[...])


def _ple_update(hn, p, w_gate, w_ple, layer, h, *, bm, bn):
    t, d = h.shape
    kp = p.shape[1]
    assert t % bm == 0 and d % bn == 0
    blk = pl.BlockSpec((bm, bn), lambda j, i: (i, j))
    return pl.pallas_call(
        _ple_kernel, grid=(d // bn, t // bm),
        in_specs=[pl.BlockSpec((bm, d), lambda j, i: (i, 0)), pl.BlockSpec((bm, kp), lambda j, i: (i, 0)),
                  pl.BlockSpec((None, d, bn), lambda j, i: (layer, 0, j)),
                  pl.BlockSpec((None, kp, bn), lambda j, i: (layer, 0, j)), blk],
        out_specs=blk, out_shape=jax.ShapeDtypeStruct((t, d), F32),
        scratch_shapes=[pltpu.VMEM((d, bn), BF16), pltpu.VMEM((kp, bn), BF16)],
        input_output_aliases={4: 0}, compiler_params=_params(2), name="ple_update",
    )(hn, p, w_gate, w_ple, h)


def _seam_rows(state, seq):
    n_seq, w, c = state.shape
    padded = jnp.pad(state, ((0, 0), (seq - w, 0), (0, 0)))
    return jnp.roll(padded, -1, axis=0).reshape(n_seq * seq, c)
def kernel(x_prompt, x_sample, p_prompt, p_sample, state_a_conv, state_delta, cache_win_k, cache_win_v, state_ffn_conv, norm_mix_pre, norm_mix_post, norm_ffn_pre, norm_ffn_post, norm_ple, w_in, a_conv_w, a_log, a_dt_bias, a_out_norm, b_sinks, c_ln_g, c_ln_b, c_w_s, c_b_s, w_br_a, w_br_b, w_br_c, w_o, w_up, ffn_conv_w, ffn_conv_b, w_down, w_ple, w_ple_gate):
    nb, ls = x_prompt.shape[:2]
    ns, lq = x_sample.shape[:2]
    tp, ts = nb * ls, ns * lq
    ple = p_prompt.shape[-1]
    h = jnp.concatenate([x_prompt.reshape(tp, D_MODEL), x_sample.reshape(ts, D_MODEL)], axis=0)
    w_in_a = w_in[:, :, :OFF_BA + LANES].astype(BF16)
    w_in_r = w_in[:, :, OFF_REST:OFF_REST + OFF_GATES].astype(BF16)
    w_in_g = w_in[:, :, OFF_REST + OFF_GATES:].astype(BF16)
    st_p, st_s = [], []
    for i in range(DEPTH):
        xn = _rms_norm_bf16(h, norm_mix_pre[i])
        proj_a = _matmul(xn, w_in_a, i, bm=BM, bn=BN_WIDE, n_cols=OFF_BA)
        proj_ba = _matmul(xn, w_in_a, i, bm=BM, bn=LANES, col0=OFF_BA, n_cols=LANES)
        proj_q = _matmul(xn, w_in_r, i, bm=BM, bn=BN_WIDE, n_cols=B_Q)
        proj_kv = _matmul(xn, w_in_r, i, bm=BM, bn=2 * B_KV, col0=OFF_KV, n_cols=2 * B_KV)
        proj_uv = _matmul(xn, w_in_r, i, bm=BM, bn=BN, col0=OFF_CUV, n_cols=2 * C_HALF)
        proj_g = _matmul(xn, w_in_g, i, bm=BM, bn=BN_WIDE)
        alog_row = jnp.zeros((1, LANES), F32).at[0, A_HEADS:2 * A_HEADS].set(a_log[i])
        dtb_row = jnp.zeros((1, LANES), F32).at[0, A_HEADS:2 * A_HEADS].set(a_dt_bias[i])
        ya_p, s_p = _delta_prompt(proj_a, proj_ba, a_conv_w[i], alog_row, dtb_row, a_out_norm[i].reshape(1, A_DV),
                                  n_seq=nb, seq_len=ls, n_heads=A_HEADS, chunk=math.gcd(ls, DELTA_CHUNK))
        abuf_p = proj_a[:tp, :A_CONV_CH].reshape(nb, ls, A_CONV_CH)[:, ls - (A_CONV - 1):]
        ya_s, s_s = _delta_sample(proj_a, proj_ba, _seam_rows(state_a_conv[i], lq), a_conv_w[i], alog_row, dtb_row,
                                  a_out_norm[i].reshape(1, A_DV), state_delta[i],
                                  row0=tp, n_seq=ns, seq=lq, n_heads=A_HEADS)
        abuf_s = proj_a[tp:, :A_CONV_CH].reshape(ns, lq, A_CONV_CH)[:, lq - (A_CONV - 1):]
        yb_p = _attn_prompt(proj_q, proj_kv, b_sinks[i], n_seq=nb, seq_len=ls)
        wb = min(WINDOW, ls)
        kv_tail = proj_kv[:tp].reshape(nb, ls, 2, B_KV_HEADS, B_HD)[:, ls - wb:]
        k_p, v_p = kv_tail[:, :, 0], kv_tail[:, :, 1]
        wbs = cache_win_k.shape[2]
        yb_s = _attn_sample(proj_q, proj_kv, cache_win_k[i].reshape(ns, wbs, B_KV), cache_win_v[i].reshape(ns, wbs, B_KV),
                            b_sinks[i], row0=tp, n_seq=ns, seq=lq)
        kv_s = proj_kv[tp:].reshape(ns, lq, 2, B_KV_HEADS, B_HD)
        k_s = jnp.concatenate([cache_win_k[i], kv_s[:, :, 0]], axis=1)[:, lq:]
        v_s = jnp.concatenate([cache_win_v[i], kv_s[:, :, 1]], axis=1)[:, lq:]
        ln_g, ln_b = c_ln_g[i].reshape(1, C_HALF), c_ln_b[i].reshape(1, C_HALF)
        yc_p = _cmlp(proj_uv, ln_g, ln_b, c_w_s[i], c_b_s[i].T, row0=0, n_rows=tp, chunk=C_CHUNK, want_vn=False)
        eye_s = jnp.eye(ns, dtype=F32)
        w_s_blk = jax.vmap(lambda w: jnp.kron(eye_s, w))(c_w_s[i][:, :lq, :lq])
        yc_s, cv_s = _cmlp(proj_uv, ln_g, ln_b, w_s_blk, jnp.tile(c_b_s[i].T[:lq], (ns, 1)),
                           row0=tp, n_rows=ts, chunk=ts, want_vn=True)
        cv_s = cv_s.reshape(ns, lq, C_HALF)
        y_a = jnp.concatenate([ya_p, ya_s])
        y_b = jnp.concatenate([yb_p, yb_s.astype(BF16)])
        y_c = jnp.concatenate([yc_p, yc_s])
        merged = _gated_merge(y_a, y_b, y_c, w_br_a, w_br_b, w_br_c, i, proj_g, bm=BM, bn=BN)
        h, xf = _residual_norm(h, _matmul(merged, w_o, i, bm=BM, bn=BN), norm_mix_post[i], norm_ffn_pre[i])
        act, tail_g, tail_u, hs_g, hs_u = _ffn_up(
            xf, w_up, i, ffn_conv_w[i], ffn_conv_b[i].reshape(1, 2 * D_FF), _seam_rows(state_ffn_conv[i], lq),
            n_seq=nb, seq_len=ls, sample_rows=ts, sample_len=lq, bm=FFN_BM, bn=FFN_BN)
        tails = jnp.concatenate([tail_g, tail_u], axis=1).reshape(nb, ls // FFN_BM, 8, 2 * D_FF)
        fb_p = tails[:, -1, 8 - (FFN_CONV - 1):]
        fb_s = jnp.concatenate([hs_g, hs_u], axis=1).reshape(ns, lq, 2 * D_FF)[:, lq - (FFN_CONV - 1):]
        f = None
        for c in range(D_FF // D_MODEL):
            f = _matmul(act, w_down, i, bm=BM, bn=BN, k_chunk=c, bk=D_MODEL, acc=f)
        h, hn = _residual_norm(h, f, norm_ffn_post[i], norm_ple[i])
        p_i = jnp.concatenate([p_prompt[i].reshape(tp, ple), p_sample[i].reshape(ts, ple)]).astype(BF16)
        h = _ple_update(hn, p_i, w_ple_gate, w_ple, i, h, bm=BM, bn=BN)
        st_p.append((abuf_p, s_p, k_p, v_p, fb_p))
        st_s.append((abuf_s, s_s, k_s, v_s, fb_s, cv_s))

    def stack(states, j):
        return jnp.stack([s[j] for s in states])

    y_prompt = h[:tp].reshape(nb, ls, D_MODEL)
    y_sample = h[tp:].reshape(ns, lq, D_MODEL)
    return (y_prompt, y_sample,
            stack(st_p, 0), stack(st_p, 1), stack(st_p, 2), stack(st_p, 3), stack(st_p, 4),
            stack(st_s, 0), stack(st_s, 1), stack(st_s, 2), stack(st_s, 3), stack(st_s, 4),
            stack(st_s, 5))
```

```python
import functools
import math

import jax
import jax.numpy as jnp
from jax import lax
from jax.experimental import pallas as pl
from jax.experimental.pallas import tpu as pltpu

D_MODEL = 4096
DEPTH = 4
A_HEADS = 8
A_DK = 128
A_DV = 128
A_QK = A_HEADS * A_DK
A_VAL = A_HEADS * A_DV
A_CONV = 4
A_CONV_CH = 2 * A_QK + A_VAL
DELTA_CHUNK = 64
B_HEADS = 16
B_KV_HEADS = 4
B_GROUP = B_HEADS // B_KV_HEADS
B_HD = 64
B_Q = B_HEADS * B_HD
B_KV = B_KV_HEADS * B_HD
WINDOW = 128
C_GROUPS = 8
C_GC = 128
C_HALF = C_GROUPS * C_GC
C_CHUNK = 128
D_FF = 3 * D_MODEL
FFN_CONV = 3
N_BRANCH = 3
EPS = 1e-6

F32 = jnp.float32
BF16 = jnp.bfloat16
LANES = 128

VMEM_LIMIT_BYTES = 56 * 1024 * 1024

DELTA_RT = 256
DELTA_HP = 4
NEUMANN_BLOCK = 16
NORM_ROWS = 264

BM = 1056
BN = 512
FFN_BM = 1024
FFN_BN = 256
OFF_BA = A_CONV_CH + A_VAL
SHIFT = 2 * A_HEADS
OFF_BQ = OFF_BA
OFF_KV = OFF_BQ + B_Q
OFF_CUV = OFF_KV + 2 * B_KV
OFF_GATES = OFF_CUV + 2 * C_HALF
BN_WIDE = 1024
CAST_ROWS = 512


def _params(n_axes):
    return pltpu.CompilerParams(dimension_semantics=("arbitrary",) * n_axes, vmem_limit_bytes=VMEM_LIMIT_BYTES)


def _sigmoid(x):
    return 1.0 / (1.0 + jnp.exp(-x))


def _softplus(x):
    return jnp.maximum(x, 0.0) + jnp.log(1.0 + jnp.exp(-jnp.abs(x)))


def _gelu_tanh(x):
    return 0.5 * x * (1.0 + jnp.tanh(0.7978845608028654 * (x + 0.044715 * (x * x * x))))


def _rms(x, g):
    return x * lax.rsqrt(jnp.mean(x * x, axis=-1, keepdims=True) + EPS) * g


def _dot(a, b):
    return jnp.dot(a, b, preferred_element_type=F32)


def _dot_nt(a, b):
    return lax.dot_general(a, b, (((1,), (1,)), ((), ())), preferred_element_type=F32)


def _dot_tn(a, b):
    return lax.dot_general(a, b, (((0,), (0,)), ((), ())), preferred_element_type=F32)


def _split2(x):
    hi = x.astype(BF16)
    lo = (x - hi.astype(F32)).astype(BF16)
    return hi, lo


def _dot3s(a, b):
    ah, al = a
    bh, bl = b
    return _dot(ah, bh) + (_dot(ah, bl) + _dot(al, bh))


def _mm_kernel(x_ref, w_ref, o_ref, wbf_ref):
    @pl.when(pl.program_id(1) == 0)
    def _():
        wbf_ref[...] = w_ref[...].astype(BF16)

    o_ref[...] = _dot(x_ref[...], wbf_ref[...])


def _mm_acc_kernel(x_ref, w_ref, a_ref, o_ref, wbf_ref):
    @pl.when(pl.program_id(1) == 0)
    def _():
        wbf_ref[...] = w_ref[...].astype(BF16)

    o_ref[...] = a_ref[...] + _dot(x_ref[...], wbf_ref[...])


def _mm_bf16_kernel(x_ref, w_ref, o_ref):
    o_ref[...] = _dot(x_ref[...], w_ref[...])


def _matmul(x, w, layer, *, bm, bn, n_cols=None, col0=0, k_chunk=0, bk=None, acc=None):
    m = x.shape[0]
    bk = w.shape[1] if bk is None else bk
    n_cols = w.shape[2] if n_cols is None else n_cols
    assert m % bm == 0 and n_cols % bn == 0 and col0 % bn == 0
    cb0 = col0 // bn
    grid = (n_cols // bn, m // bm)
    x_spec = pl.BlockSpec((bm, bk), lambda j, i: (i, k_chunk))
    w_spec = pl.BlockSpec((None, bk, bn), lambda j, i: (layer, k_chunk, cb0 + j))
    o_spec = pl.BlockSpec((bm, bn), lambda j, i: (i, j))
    out_shape = jax.ShapeDtypeStruct((m, n_cols), F32)
    if w.dtype == BF16:
        assert acc is None
        return pl.pallas_call(_mm_bf16_kernel, grid=grid, in_specs=[x_spec, w_spec], out_specs=o_spec,
                              out_shape=out_shape, compiler_params=_params(2), name="proj_bf16")(x, w)
    scratch = [pltpu.VMEM((bk, bn), BF16)]
    if acc is None:
        return pl.pallas_call(_mm_kernel, grid=grid, in_specs=[x_spec, w_spec], out_specs=o_spec,
                              out_shape=out_shape, scratch_shapes=scratch, compiler_params=_params(2),
                              name="proj")(x, w)
    return pl.pallas_call(_mm_acc_kernel, grid=grid, in_specs=[x_spec, w_spec, o_spec], out_specs=o_spec,
                          out_shape=out_shape, scratch_shapes=scratch, compiler_params=_params(2),
                          input_output_aliases={2: 0}, name="proj_acc")(x, w, acc)


def _mm_shift_kernel(x_ref, wa_ref, wb_ref, o_ref, wsh_ref, *, shift):
    @pl.when(pl.program_id(1) == 0)
    def _():
        k, bn = wa_ref.shape
        for r0 in range(0, k, CAST_ROWS):
            rs = slice(r0, r0 + CAST_ROWS)
            w = jnp.concatenate([wa_ref[rs, :], wb_ref[rs, :]], axis=1).astype(F32)
            wsh_ref[rs, :] = pltpu.roll(w, w.shape[1] - shift, axis=1)[:, :bn].astype(BF16)

    o_ref[...] = _dot(x_ref[...], wsh_ref[...])


def _matmul_shifted(x, w, layer, *, bm, bn, col0, n_cols, shift):
    m, k = x.shape
    assert m % bm == 0 and n_cols % bn == 0 and col0 % bn == 0 and 0 < shift < LANES and k % CAST_ROWS == 0
    assert w.dtype == BF16
    cb0 = col0 // bn
    lpb = bn // LANES
    return pl.pallas_call(
        functools.partial(_mm_shift_kernel, shift=shift), grid=(n_cols // bn, m // bm),
        in_specs=[pl.BlockSpec((bm, k), lambda j, i: (i, 0)),
                  pl.BlockSpec((None, k, bn), lambda j, i: (layer, 0, cb0 + j)),
                  pl.BlockSpec((None, k, LANES), lambda j, i: (layer, 0, (cb0 + j + 1) * lpb))],
        out_specs=pl.BlockSpec((bm, bn), lambda j, i: (i, j)),
        out_shape=jax.ShapeDtypeStruct((m, n_cols), F32),
        scratch_shapes=[pltpu.VMEM((k, bn), BF16)],
        compiler_params=_params(2), name="proj_shift",
    )(x, w, w)


def _norm_kernel(h_ref, g_ref, xn_ref):
    xn_ref[...] = _rms(h_ref[...], g_ref[...]).astype(xn_ref.dtype)


def _resnorm_kernel(h_ref, y_ref, gp_ref, gn_ref, ho_ref, xn_ref):
    h = h_ref[...] + _rms(y_ref[...], gp_ref[...])
    ho_ref[...] = h
    xn_ref[...] = _rms(h, gn_ref[...]).astype(xn_ref.dtype)


def _rms_norm_bf16(h, g):
    t, d = h.shape
    assert t % NORM_ROWS == 0
    row = pl.BlockSpec((NORM_ROWS, d), lambda r: (r, 0))
    vec = pl.BlockSpec((1, d), lambda r: (0, 0))
    return pl.pallas_call(_norm_kernel, grid=(t // NORM_ROWS,), in_specs=[row, vec], out_specs=row,
                          out_shape=jax.ShapeDtypeStruct((t, d), BF16), compiler_params=_params(1),
                          name="rms_norm")(h, g.reshape(1, d))


def _residual_norm(h, y, g_post, g_next):
    t, d = h.shape
    assert t % NORM_ROWS == 0
    row = pl.BlockSpec((NORM_ROWS, d), lambda r: (r, 0))
    vec = pl.BlockSpec((1, d), lambda r: (0, 0))
    return pl.pallas_call(
        _resnorm_kernel, grid=(t // NORM_ROWS,), in_specs=[row, row, vec, vec], out_specs=[row, row],
        out_shape=[jax.ShapeDtypeStruct((t, d), F32), jax.ShapeDtypeStruct((t, d), BF16)],
        input_output_aliases={0: 0}, compiler_params=_params(1), name="residual_norm",
    )(h, y, g_post.reshape(1, d), g_next.reshape(1, d))


def _run(gen):
    try:
        while True:
            next(gen)
    except StopIteration as stop:
        return stop.value


def _lockstep(gens):
    results = [None] * len(gens)
    live = list(range(len(gens)))
    while live:
        still = []
        for idx in live:
            try:
                next(gens[idx])
                still.append(idx)
            except StopIteration as stop:
                results[idx] = stop.value
        live = still
    return results


def _unit_lower_inverse(a, rx, c):
    eye = (rx == 0).astype(F32)
    blk = min(NEUMANN_BLOCK, c)
    n = -jnp.where(rx < blk, a, 0.0)
    p = eye + n
    pw = n
    for _ in range(int(math.log2(blk)) - 1):
        pws = _split2(pw)
        pw = _dot3s(pws, pws)
        yield
        p = p + _dot3s(_split2(p), _split2(pw))
        yield
    while blk < c:
        e = jnp.where((rx >= blk) & (rx < 2 * blk), a, 0.0)
        ps = _split2(p)
        ep = _dot3s(_split2(e), ps)
        yield
        p = p - _dot3s(ps, _split2(ep))
        yield
        blk *= 2
    return p


def _shift_rows(x, prev, j, pos):
    rp = pltpu.roll(prev, j, axis=0)
    if prev.shape[0] != x.shape[0]:
        rp = jnp.broadcast_to(rp[None], (x.shape[0] // 8, 8, x.shape[1])).reshape(x.shape)
    return jnp.where(pos < j, rp, pltpu.roll(x, j, axis=0))


def _conv_silu(x, xprev, cw, seg):
    pos = lax.broadcasted_iota(jnp.int32, x.shape, 0) & (seg - 1)
    acc = x * cw[3:4, :]
    for j in range(1, 4):
        acc = acc + _shift_rows(x, xprev, j, pos) * cw[3 - j:4 - j, :]
    return acc * _sigmoid(acc)


def _delta_tile_prep(q_raw, k_raw, v_raw, pq, pk, pv, cwq, cwk, cwv, beta_col, g_col, c, seg):
    rt = q_raw.shape[0]
    q = _conv_silu(q_raw, pq, cwq, seg)
    k = _conv_silu(k_raw, pk, cwk, seg)
    v = _conv_silu(v_raw, pv, cwv, seg)
    q = q * (lax.rsqrt(jnp.sum(q * q, axis=-1, keepdims=True) + EPS) * (A_DK ** -0.5))
    k = k * lax.rsqrt(jnp.sum(k * k, axis=-1, keepdims=True) + EPS)
    yield

    r_i = lax.broadcasted_iota(jnp.int32, (rt, rt), 0)
    c_i = lax.broadcasted_iota(jnp.int32, (rt, rt), 1)
    rx = r_i ^ c_i
    same = rx < c
    lower = same & (c_i <= r_i)
    strict = same & (c_i < r_i)

    g1 = g_col.astype(BF16)
    r1 = g_col - g1.astype(F32)
    g2 = r1.astype(BF16)
    g3 = (r1 - g2.astype(F32)).astype(BF16)
    lmat = lower.astype(BF16)
    umat = (same & (r_i <= c_i)).astype(BF16)
    ones = jnp.ones((rt, rt), BF16)
    gc = None
    gr = None
    for gp in (g1, g2, g3):
        a = _dot(lmat, jnp.broadcast_to(gp, (rt, LANES)))
        b = _dot(ones, jnp.broadcast_to(gp, (rt, rt)) * umat)
        gc = a if gc is None else gc + a
        gr = b if gr is None else gr + b
    gc_full = jnp.concatenate([gc] * (rt // LANES), axis=1)
    decay = jnp.where(lower, jnp.exp(jnp.where(lower, gc_full - gr, 0.0)), 0.0)
    yield

    kb = k.astype(BF16)
    kk = _dot_nt(kb, kb)
    a_mat = jnp.where(strict, beta_col * kk * decay, 0.0)
    yield
    t_inv = yield from _unit_lower_inverse(a_mat, rx, c)
    rhs = jnp.concatenate([beta_col * v, (beta_col * jnp.exp(gc)) * k], axis=1)
    sol = _dot3s(_split2(t_inv), _split2(rhs))
    yield
    u_base = sol[:, :LANES]
    w_mat = sol[:, LANES:]
    qk = jnp.where(lower, _dot_nt(q.astype(BF16), kb) * decay, 0.0)
    q_dec = q * jnp.exp(gc)
    yield
    return k, gc, u_base, w_mat, qk, q_dec


def _gated_out_norm(o, z, og):
    return _rms(o, og) * (z * _sigmoid(z))


def _head_cols(ba, alog_row, dtb_row, h, n_heads):
    lane = lax.broadcasted_iota(jnp.int32, ba.shape, 1)
    sig = _sigmoid(ba)
    g_all = -jnp.exp(alog_row) * _softplus(ba + dtb_row)
    beta_col = jnp.sum(jnp.where(lane == h, sig, 0.0), axis=1, keepdims=True)
    g_col = jnp.sum(jnp.where(lane == n_heads + h, g_all, 0.0), axis=1, keepdims=True)
    return beta_col, g_col


def _delta_prompt_kernel(q_ref, k_ref, v_ref, z_ref, ba_ref, cwq_ref, cwk_ref, cwv_ref, alog_ref, dtb_ref, og_ref,
                         y_ref, sfin_ref, xprev_ref, s_ref, *, n_heads, chunk):
    j = pl.program_id(2)
    hp = pl.program_id(1)
    rt = q_ref.shape[0]

    @pl.when(j == 0)
    def _():
        xprev_ref[...] = jnp.zeros_like(xprev_ref)
        s_ref[...] = jnp.zeros_like(s_ref)

    ba = ba_ref[...]

    def head(hh):
        cs = slice(hh * LANES, (hh + 1) * LANES)
        beta_col, g_col = _head_cols(ba, alog_ref[...], dtb_ref[...], hp * DELTA_HP + hh, n_heads)
        q_raw, k_raw, v_raw = q_ref[:, cs], k_ref[:, cs], v_ref[:, cs]
        k, gc, u_base, w_mat, qk, q_dec = yield from _delta_tile_prep(
            q_raw, k_raw, v_raw, xprev_ref[0, :, cs], xprev_ref[1, :, cs], xprev_ref[2, :, cs],
            cwq_ref[:, cs], cwk_ref[:, cs], cwv_ref[:, cs], beta_col, g_col, chunk, rt)
        xprev_ref[0, :, cs] = q_raw[rt - 8:]
        xprev_ref[1, :, cs] = k_raw[rt - 8:]
        xprev_ref[2, :, cs] = v_raw[rt - 8:]
        s = s_ref[hh]
        us, os_ = [], []
        for cc in range(rt // chunk):
            rs = slice(cc * chunk, (cc + 1) * chunk)
            sb = s.astype(BF16)
            u_c = u_base[rs] - _dot(w_mat[rs].astype(BF16), sb)
            os_.append(_dot(q_dec[rs].astype(BF16), sb))
            yield
            gl = gc[(cc + 1) * chunk - 1:(cc + 1) * chunk, :]
            k_dec = k[rs] * jnp.exp(gl - gc[rs])
            s = jnp.exp(gl) * s + _dot_tn(k_dec.astype(BF16), u_c.astype(BF16))
            us.append(u_c)
            yield
        s_ref[hh] = s
        u_all = jnp.concatenate(us, axis=0)
        o = jnp.concatenate(os_, axis=0) + _dot(qk.astype(BF16), u_all.astype(BF16))
        yield
        y_ref[:, cs] = _gated_out_norm(o, z_ref[:, cs], og_ref[...]).astype(y_ref.dtype)

    _lockstep([head(hh) for hh in range(DELTA_HP)])

    @pl.when(j == pl.num_programs(2) - 1)
    def _():
        sfin_ref[0] = s_ref[...]


def _delta_prompt(proj_a, proj_ba, conv_w, alog_row, dtb_row, out_g, *, n_seq, seq_len, n_heads, chunk):
    rt, hp = DELTA_RT, DELTA_HP
    assert seq_len % rt == 0 and rt % chunk == 0 and n_heads % hp == 0
    tiles = seq_len // rt
    nhp = n_heads // hp
    wblk = hp * LANES

    def col_spec(base):
        return pl.BlockSpec((rt, wblk), lambda n, h, j: (n * tiles + j, base + h))

    def cw_spec(base):
        return pl.BlockSpec((4, wblk), lambda n, h, j: (0, base + h))

    row128 = pl.BlockSpec((1, LANES), lambda n, h, j: (0, 0))
    kern = functools.partial(_delta_prompt_kernel, n_heads=n_heads, chunk=chunk)
    return pl.pallas_call(
        kern, grid=(n_seq, nhp, tiles),
        in_specs=[col_spec(0), col_spec(nhp), col_spec(2 * nhp), col_spec(3 * nhp),
                  pl.BlockSpec((rt, LANES), lambda n, h, j: (n * tiles + j, 0)),
                  cw_spec(0), cw_spec(nhp), cw_spec(2 * nhp), row128, row128, row128],
        out_specs=[pl.BlockSpec((rt, wblk), lambda n, h, j: (n * tiles + j, h)),
                   pl.BlockSpec((1, hp, LANES, LANES), lambda n, h, j: (n, h, 0, 0))],
        out_shape=[jax.ShapeDtypeStruct((n_seq * seq_len, n_heads * LANES), BF16),
                   jax.ShapeDtypeStruct((n_seq, n_heads, LANES, LANES), F32)],
        scratch_shapes=[pltpu.VMEM((3, 8, wblk), F32), pltpu.VMEM((hp, LANES, LANES), F32)],
        compiler_params=_params(3), name="delta_prompt",
    )(proj_a, proj_a, proj_a, proj_a, proj_ba, conv_w, conv_w, conv_w, alog_row, dtb_row, out_g)


def _attn_prompt_kernel(q_ref, kc_ref, kp_ref, vc_ref, vp_ref, sk_ref, o_ref, *, blocks_per_seq):
    b = pl.program_id(0) % blocks_per_seq
    w = q_ref.shape[0]
    r_i = lax.broadcasted_iota(jnp.int32, (w, 2 * w), 0)
    c_i = lax.broadcasted_iota(jnp.int32, (w, 2 * w), 1)
    mask = (c_i > r_i) & (c_i <= r_i + w) & ((b > 0) | (c_i >= w))
    scale = B_HD ** -0.5
    for kvh in range(B_KV_HEADS):
        ks = slice(kvh * B_HD, (kvh + 1) * B_HD)
        kk = jnp.concatenate([kp_ref[:, ks], kc_ref[:, ks]], axis=0).astype(BF16)
        vv = jnp.concatenate([vp_ref[:, ks], vc_ref[:, ks]], axis=0).astype(BF16)
        for g in range(B_GROUP):
            h = kvh * B_GROUP + g
            hs = slice(h * B_HD, (h + 1) * B_HD)
            s = _dot_nt(q_ref[:, hs].astype(BF16), kk) * scale
            s = jnp.where(mask, s, -jnp.inf)
            sink = sk_ref[h]
            m = jnp.maximum(jnp.max(s, axis=-1, keepdims=True), sink)
            e = jnp.exp(s - m)
            p = e / (jnp.sum(e, axis=-1, keepdims=True) + jnp.exp(sink - m))
            o_ref[:, hs] = _dot(p.astype(BF16), vv).astype(o_ref.dtype)


def _attn_prompt(q_arr, kv_arr, sinks, *, n_seq, seq_len):
    w = WINDOW
    bps = seq_len // w
    cur = lambda r: (r, 0)
    return pl.pallas_call(
        functools.partial(_attn_prompt_kernel, blocks_per_seq=bps), grid=(n_seq * bps,),
        in_specs=[pl.BlockSpec((w, B_Q), cur),
                  pl.BlockSpec((w, B_KV), cur), pl.BlockSpec((w, B_KV), lambda r: (jnp.maximum(r - 1, 0), 0)),
                  pl.BlockSpec((w, B_KV), lambda r: (r, 1)), pl.BlockSpec((w, B_KV), lambda r: (jnp.maximum(r - 1, 0), 1)),
                  pl.BlockSpec(memory_space=pltpu.SMEM)],
        out_specs=pl.BlockSpec((w, B_Q), cur),
        out_shape=jax.ShapeDtypeStruct((n_seq * seq_len, B_Q), BF16),
        compiler_params=_params(1), name="attn_prompt",
    )(q_arr, kv_arr, kv_arr, kv_arr, kv_arr, sinks)


def _cmlp_kernel(u_ref, v_ref, lg_ref, lb_ref, ws_ref, bs_ref, o_ref, *vn_ref):
    v = _gelu_tanh(v_ref[...])
    mu = jnp.mean(v, axis=-1, keepdims=True)
    d = v - mu
    var = jnp.mean(d * d, axis=-1, keepdims=True)
    vn = d * lax.rsqrt(var + EPS) * lg_ref[...] + lb_ref[...]
    if vn_ref:
        vn_ref[0][...] = vn
    c = v.shape[0]
    tril = lax.broadcasted_iota(jnp.int32, (c, c), 1) <= lax.broadcasted_iota(jnp.int32, (c, c), 0)
    for g in range(C_GROUPS):
        gs = slice(g * C_GC, (g + 1) * C_GC)
        wc = jnp.where(tril, ws_ref[g], 0.0).astype(BF16)
        mixed = _dot(wc, vn[:, gs].astype(BF16)) + bs_ref[:, g:g + 1]
        o_ref[:, gs] = (_gelu_tanh(u_ref[:, gs]) * mixed).astype(o_ref.dtype)


def _cmlp(uv_arr, ln_g, ln_b, w_s, b_s_t, *, row0, n_rows, chunk, want_vn):
    assert row0 % chunk == 0 and n_rows % chunk == 0
    rb0 = row0 // chunk
    blk = pl.BlockSpec((chunk, C_HALF), lambda r: (r, 0))
    vec = pl.BlockSpec((1, C_HALF), lambda r: (0, 0))
    out_specs = [blk, blk] if want_vn else blk
    y_shape = jax.ShapeDtypeStruct((n_rows, C_HALF), BF16)
    out_shape = [y_shape, jax.ShapeDtypeStruct((n_rows, C_HALF), F32)] if want_vn else y_shape
    return pl.pallas_call(
        _cmlp_kernel, grid=(n_rows // chunk,),
        in_specs=[pl.BlockSpec((chunk, C_HALF), lambda r: (rb0 + r, 0)),
                  pl.BlockSpec((chunk, C_HALF), lambda r: (rb0 + r, 1)), vec, vec,
                  pl.BlockSpec((C_GROUPS, chunk, chunk), lambda r: (0, 0, 0)),
                  pl.BlockSpec((chunk, C_GROUPS), lambda r: (0, 0))],
        out_specs=out_specs, out_shape=out_shape, compiler_params=_params(1), name="cmlp",
    )(uv_arr, uv_arr, ln_g, ln_b, w_s, b_s_t)


def _delta_sample_kernel(q_ref, k_ref, v_ref, z_ref, ba_ref, pq_ref, pk_ref, pv_ref, cwq_ref, cwk_ref, cwv_ref,
                         alog_ref, dtb_ref, og_ref, s0_ref, y_ref, snew_ref,
                         ub_ref, wm_ref, qd_ref, kd_ref, eg_ref, u_ref, oi_ref, *, n_heads, seq):
    h = pl.program_id(0)
    rt = q_ref.shape[0]
    beta_col, g_col = _head_cols(ba_ref[...], alog_ref[...], dtb_ref[...], h, n_heads)
    k, gc, u_base, w_mat, qk, q_dec = _run(_delta_tile_prep(
        q_ref[...], k_ref[...], v_ref[...], pq_ref[...], pk_ref[...], pv_ref[...],
        cwq_ref[...], cwk_ref[...], cwv_ref[...], beta_col, g_col, seq, seq))
    r_i = lax.broadcasted_iota(jnp.int32, (rt, rt), 0)
    c_i = lax.broadcasted_iota(jnp.int32, (rt, rt), 1)
    sel = (c_i == (r_i | (seq - 1))).astype(BF16)
    g1 = gc.astype(BF16)
    r1 = gc - g1.astype(F32)
    g2 = r1.astype(BF16)
    g3 = (r1 - g2.astype(F32)).astype(BF16)
    gl = _dot(sel, g1) + (_dot(sel, g2) + _dot(sel, g3))
    pad = jnp.zeros((8, LANES), F32)
    for ref, val in ((ub_ref, u_base), (wm_ref, w_mat), (qd_ref, q_dec), (kd_ref, k * jnp.exp(gl - gc)),
                     (eg_ref, jnp.exp(gl))):
        ref[0:rt, :] = val
        ref[rt:rt + 8, :] = pad

    def body(s, carry):
        r0 = pl.multiple_of(s * seq, seq)
        win = pl.ds(r0, 2 * seq)
        s0 = s0_ref[s]
        sb = s0.astype(BF16)
        first = lax.broadcasted_iota(jnp.int32, (2 * seq, LANES), 0) < seq
        u16 = ub_ref[win, :] - _dot(wm_ref[win, :].astype(BF16), sb)
        o16 = _dot(qd_ref[win, :].astype(BF16), sb)
        kd16 = jnp.where(first, kd_ref[win, :], 0.0)
        snew_ref[s] = eg_ref[pl.ds(r0, 1), :] * s0 + _dot_tn(kd16.astype(BF16), jnp.where(first, u16, 0.0).astype(BF16))
        u_ref[pl.ds(r0, seq), :] = u16[0:seq]
        oi_ref[pl.ds(r0, seq), :] = o16[0:seq]
        return carry

    lax.fori_loop(0, rt // seq, body, 0)
    o = oi_ref[...] + _dot(qk.astype(BF16), u_ref[...].astype(BF16))
    y_ref[...] = _gated_out_norm(o, z_ref[...], og_ref[...]).astype(y_ref.dtype)


def _delta_sample(proj_a, proj_ba, p_conv, conv_w, alog_row, dtb_row, out_g, s0, *, row0, n_seq, seq, n_heads):
    rt = n_seq * seq
    assert seq == 8 and row0 % rt == 0 and rt % LANES == 0
    rb = row0 // rt

    def col_spec(base):
        return pl.BlockSpec((rt, LANES), lambda h: (rb, base + h))

    def p_spec(base):
        return pl.BlockSpec((rt, LANES), lambda h: (0, base + h))

    def cw_spec(base):
        return pl.BlockSpec((4, LANES), lambda h: (0, base + h))

    row128 = pl.BlockSpec((1, LANES), lambda h: (0, 0))
    st_spec = pl.BlockSpec((n_seq, None, LANES, LANES), lambda h: (0, h, 0, 0))
    big = pltpu.VMEM((rt + 8, LANES), F32)
    kern = functools.partial(_delta_sample_kernel, n_heads=n_heads, seq=seq)
    return pl.pallas_call(
        kern, grid=(n_heads,),
        in_specs=[col_spec(0), col_spec(n_heads), col_spec(2 * n_heads), col_spec(3 * n_heads),
                  pl.BlockSpec((rt, LANES), lambda h: (rb, 0)),
                  p_spec(0), p_spec(n_heads), p_spec(2 * n_heads),
                  cw_spec(0), cw_spec(n_heads), cw_spec(2 * n_heads), row128, row128, row128, st_spec],
        out_specs=[pl.BlockSpec((rt, LANES), lambda h: (0, h)), st_spec],
        out_shape=[jax.ShapeDtypeStruct((rt, n_heads * LANES), BF16),
                   jax.ShapeDtypeStruct((n_seq, n_heads, LANES, LANES), F32)],
        scratch_shapes=[big, big, big, big, big, pltpu.VMEM((rt, LANES), F32), pltpu.VMEM((rt, LANES), F32)],
        compiler_params=_params(1), name="delta_sample",
    )(proj_a, proj_a, proj_a, proj_a, proj_ba, p_conv, p_conv, p_conv, conv_w, conv_w, conv_w,
      alog_row, dtb_row, out_g, s0)


def _attn_sample_kernel(q_ref, kvn_ref, kb_ref, vb_ref, sk_ref, o_ref):
    lq = q_ref.shape[0]
    wb = kb_ref.shape[0]
    nk = 2 * wb
    rows = B_GROUP * lq
    r_i = lax.broadcasted_iota(jnp.int32, (rows, nk), 0)
    c_i = lax.broadcasted_iota(jnp.int32, (rows, nk), 1)
    i = r_i & (lq - 1)
    mask = (c_i <= i + wb) & (c_i > i + wb - WINDOW)
    grp = lax.shift_right_logical(lax.broadcasted_iota(jnp.int32, (rows, 1), 0), int(math.log2(lq)))
    scale = B_HD ** -0.5
    zpad = jnp.zeros((nk - wb - lq, B_HD), F32)
    for kvh in range(B_KV_HEADS):
        ks = slice(kvh * B_HD, (kvh + 1) * B_HD)
        vs = slice(B_KV + kvh * B_HD, B_KV + (kvh + 1) * B_HD)
        kk = jnp.concatenate([kb_ref[:, ks], kvn_ref[:, ks], zpad], axis=0).astype(BF16)
        vv = jnp.concatenate([vb_ref[:, ks], kvn_ref[:, vs], zpad], axis=0).astype(BF16)
        h0 = kvh * B_GROUP
        q = jnp.concatenate([q_ref[:, (h0 + g) * B_HD:(h0 + g + 1) * B_HD] for g in range(B_GROUP)], axis=0)
        s = jnp.where(mask, _dot_nt(q.astype(BF16), kk) * scale, -jnp.inf)
        sink = jnp.zeros((rows, 1), F32)
        for g in range(B_GROUP):
            sink = jnp.where(grp == g, sk_ref[h0 + g], sink)
        m = jnp.maximum(jnp.max(s, axis=-1, keepdims=True), sink)
        e = jnp.exp(s - m)
        p = e / (jnp.sum(e, axis=-1, keepdims=True) + jnp.exp(sink - m))
        o = _dot(p.astype(BF16), vv)
        for g in range(B_GROUP):
            o_ref[:, (h0 + g) * B_HD:(h0 + g + 1) * B_HD] = o[g * lq:(g + 1) * lq]


def _attn_sample(q_arr, kv_arr, k_buf, v_buf, sinks, *, row0, n_seq, seq):
    assert row0 % seq == 0 and seq & (seq - 1) == 0 and k_buf.shape[1] >= seq
    rb0 = row0 // seq
    wb = k_buf.shape[1]
    buf = pl.BlockSpec((None, wb, B_KV), lambda s: (s, 0, 0))
    return pl.pallas_call(
        _attn_sample_kernel, grid=(n_seq,),
        in_specs=[pl.BlockSpec((seq, B_Q), lambda s: (rb0 + s, 0)), pl.BlockSpec((seq, 2 * B_KV), lambda s: (rb0 + s, 0)),
                  buf, buf, pl.BlockSpec(memory_space=pltpu.SMEM)],
        out_specs=pl.BlockSpec((seq, B_Q), lambda s: (s, 0)),
        out_shape=jax.ShapeDtypeStruct((n_seq * seq, B_Q), F32),
        compiler_params=_params(1), name="attn_sample",
    )(q_arr, kv_arr, k_buf, v_buf, sinks)


def _merge_kernel(ya_ref, yb_ref, yc_ref, wa_ref, wb_ref, wc_ref, ga_ref, gb_ref, gc_ref, o_ref,
                  wab_ref, wbb_ref, wcb_ref):
    @pl.when(pl.program_id(1) == 0)
    def _():
        wab_ref[...] = wa_ref[...].astype(BF16)
        wbb_ref[...] = wb_ref[...].astype(BF16)
        wcb_ref[...] = wc_ref[...].astype(BF16)

    acc = _sigmoid(ga_ref[...]) * _dot(ya_ref[...], wab_ref[...])
    acc = acc + _sigmoid(gb_ref[...]) * _dot(yb_ref[...], wbb_ref[...])
    acc = acc + _sigmoid(gc_ref[...]) * _dot(yc_ref[...], wcb_ref[...])
    o_ref[...] = acc.astype(o_ref.dtype)


def _gated_merge(y_a, y_b, y_c, w_a, w_b, w_c, layer, gate_logits, *, bm, bn):
    t, kdim = y_a.shape
    d = w_a.shape[2]
    assert t % bm == 0 and d % bn == 0
    nj = d // bn
    y_spec = pl.BlockSpec((bm, kdim), lambda j, i: (i, 0))
    w_spec = pl.BlockSpec((None, kdim, bn), lambda j, i: (layer, 0, j))

    def g_spec(b):
        return pl.BlockSpec((bm, bn), lambda j, i: (i, b * nj + j))

    return pl.pallas_call(
        _merge_kernel, grid=(nj, t // bm),
        in_specs=[y_spec, y_spec, y_spec, w_spec, w_spec, w_spec, g_spec(0), g_spec(1), g_spec(2)],
        out_specs=pl.BlockSpec((bm, bn), lambda j, i: (i, j)),
        out_shape=jax.ShapeDtypeStruct((t, d), BF16),
        scratch_shapes=[pltpu.VMEM((kdim, bn), BF16)] * 3,
        compiler_params=_params(2), name="gated_merge",
    )(y_a, y_b, y_c, w_a, w_b, w_c, gate_logits, gate_logits, gate_logits)


def _conv3(h, hprev, cw, b, pos):
    acc = h * cw[2:3, :] + b
    for j in (1, 2):
        acc = acc + _shift_rows(h, hprev, j, pos) * cw[2 - j:3 - j, :]
    return acc


def _ffn_up_kernel(x_ref, wg_ref, wu_ref, cwg_ref, cwu_ref, bg_ref, bu_ref, pg_ref, pu_ref,
                   act_ref, tg_ref, tu_ref, hsg_ref, hsu_ref, wgb_ref, wub_ref, prevg_ref, prevu_ref,
                   *, tiles_per_seq, n_prompt_tiles, sample_rows, sample_len):
    m = pl.program_id(1)
    bm = x_ref.shape[0]

    @pl.when(m == 0)
    def _():
        wgb_ref[...] = wg_ref[...].astype(BF16)
        wub_ref[...] = wu_ref[...].astype(BF16)

    @pl.when(m < n_prompt_tiles)
    def _():
        x = x_ref[...]
        hg = _dot(x, wgb_ref[...])
        hu = _dot(x, wub_ref[...])
        first = (m % tiles_per_seq) == 0
        pos = lax.broadcasted_iota(jnp.int32, hg.shape, 0)
        pg = jnp.where(first, 0.0, prevg_ref[...])
        pu = jnp.where(first, 0.0, prevu_ref[...])
        g = _conv3(hg, pg, cwg_ref[...], bg_ref[...], pos)
        u = _conv3(hu, pu, cwu_ref[...], bu_ref[...], pos)
        act_ref[...] = (_gelu_tanh(g) * u).astype(act_ref.dtype)
        prevg_ref[...] = hg
        prevu_ref[...] = hu
        tg_ref[...] = hg[bm - 8:, :]
        tu_ref[...] = hu[bm - 8:, :]

    @pl.when(m == n_prompt_tiles)
    def _():
        x = x_ref[0:sample_rows, :]
        hg = _dot(x, wgb_ref[...])
        hu = _dot(x, wub_ref[...])
        pos = lax.broadcasted_iota(jnp.int32, hg.shape, 0) & (sample_len - 1)
        g = _conv3(hg, pg_ref[...], cwg_ref[...], bg_ref[...], pos)
        u = _conv3(hu, pu_ref[...], cwu_ref[...], bu_ref[...], pos)
        act_ref[0:sample_rows, :] = (_gelu_tanh(g) * u).astype(act_ref.dtype)
        hsg_ref[...] = hg
        hsu_ref[...] = hu


def _ffn_up(x, w_up, layer, conv_w, conv_b, p_state, *, n_seq, seq_len, sample_rows, sample_len, bm, bn):
    t, d = x.shape
    f = w_up.shape[2] // 2
    tp = n_seq * seq_len
    assert seq_len % bm == 0 and f % bn == 0 and t == tp + sample_rows and sample_rows <= bm
    npt = tp // bm
    nj = f // bn
    kern = functools.partial(_ffn_up_kernel, tiles_per_seq=seq_len // bm, n_prompt_tiles=npt,
                             sample_rows=sample_rows, sample_len=sample_len)

    def half(rows, base):
        return pl.BlockSpec((rows, bn), lambda j, m: (0, base + j))

    def w_half(base):
        return pl.BlockSpec((None, d, bn), lambda j, m: (layer, 0, base + j))

    tail_spec = pl.BlockSpec((8, bn), lambda j, m: (jnp.minimum(m, npt - 1), j))
    return pl.pallas_call(
        kern, grid=(nj, npt + 1),
        in_specs=[pl.BlockSpec((bm, d), lambda j, m: (m, 0)),
                  w_half(0), w_half(nj), half(3, 0), half(3, nj), half(1, 0), half(1, nj),
                  half(sample_rows, 0), half(sample_rows, nj)],
        out_specs=[pl.BlockSpec((bm, bn), lambda j, m: (m, j)), tail_spec, tail_spec,
                   half(sample_rows, 0), half(sample_rows, 0)],
        out_shape=[jax.ShapeDtypeStruct((t, f), BF16),
                   jax.ShapeDtypeStruct((npt * 8, f), F32), jax.ShapeDtypeStruct((npt * 8, f), F32),
                   jax.ShapeDtypeStruct((sample_rows, f), F32), jax.ShapeDtypeStruct((sample_rows, f), F32)],
        scratch_shapes=[pltpu.VMEM((d, bn), BF16), pltpu.VMEM((d, bn), BF16),
                        pltpu.VMEM((bm, bn), F32), pltpu.VMEM((bm, bn), F32)],
        compiler_params=_params(2), name="ffn_up",
    )(x, w_up, w_up, conv_w, conv_w, conv_b, conv_b, p_state, p_state)


def _ple_kernel(x_ref, p_ref, wg_ref, wp_ref, h_ref, o_ref, wgb_ref, wpb_ref):
    @pl.when(pl.program_id(1) == 0)
    def _():
        wgb_ref[...] = wg_ref[...].astype(BF16)
        wpb_ref[...] = wp_ref[...].astype(BF16)

    gate = _sigmoid(_dot(x_ref[...], wgb_ref[...]))
    o_ref[...] = h_ref[...] + gate * _dot(p_ref[...], wpb_ref[...])


def _ple_update(hn, p, w_gate, w_ple, layer, h, *, bm, bn):
    t, d = h.shape
    kp = p.shape[1]
    assert t % bm == 0 and d % bn == 0
    blk = pl.BlockSpec((bm, bn), lambda j, i: (i, j))
    return pl.pallas_call(
        _ple_kernel, grid=(d // bn, t // bm),
        in_specs=[pl.BlockSpec((bm, d), lambda j, i: (i, 0)), pl.BlockSpec((bm, kp), lambda j, i: (i, 0)),
                  pl.BlockSpec((None, d, bn), lambda j, i: (layer, 0, j)),
                  pl.BlockSpec((None, kp, bn), lambda j, i: (layer, 0, j)), blk],
        out_specs=blk, out_shape=jax.ShapeDtypeStruct((t, d), F32),
        scratch_shapes=[pltpu.VMEM((d, bn), BF16), pltpu.VMEM((kp, bn), BF16)],
        input_output_aliases={4: 0}, compiler_params=_params(2), name="ple_update",
    )(hn, p, w_gate, w_ple, h)


def _seam_rows(state, seq):
    n_seq, w, c = state.shape
    padded = jnp.pad(state, ((0, 0), (seq - w, 0), (0, 0)))
    return jnp.roll(padded, -1, axis=0).reshape(n_seq * seq, c)
def kernel(x_prompt, x_sample, p_prompt, p_sample, state_a_conv, state_delta, cache_win_k, cache_win_v, state_ffn_conv, norm_mix_pre, norm_mix_post, norm_ffn_pre, norm_ffn_post, norm_ple, w_in, a_conv_w, a_log, a_dt_bias, a_out_norm, b_sinks, c_ln_g, c_ln_b, c_w_s, c_b_s, w_br_a, w_br_b, w_br_c, w_o, w_up, ffn_conv_w, ffn_conv_b, w_down, w_ple, w_ple_gate):
    nb, ls = x_prompt.shape[:2]
    ns, lq = x_sample.shape[:2]
    tp, ts = nb * ls, ns * lq
    ple = p_prompt.shape[-1]
    h = jnp.concatenate([x_prompt.reshape(tp, D_MODEL), x_sample.reshape(ts, D_MODEL)], axis=0)
    w_in_bf = w_in.astype(BF16)
    shifted = functools.partial(_matmul_shifted, bm=BM, bn=BN, shift=SHIFT)
    st_p, st_s = [], []
    for i in range(DEPTH):
        xn = _rms_norm_bf16(h, norm_mix_pre[i])
        proj_a = _matmul(xn, w_in_bf, i, bm=BM, bn=BN_WIDE, n_cols=OFF_BA)
        proj_ba = _matmul(xn, w_in_bf, i, bm=BM, bn=LANES, col0=OFF_BA, n_cols=LANES)
        proj_q = shifted(xn, w_in_bf, i, col0=OFF_BQ, n_cols=B_Q)
        proj_kv = shifted(xn, w_in_bf, i, col0=OFF_KV, n_cols=2 * B_KV)
        proj_uv = shifted(xn, w_in_bf, i, col0=OFF_CUV, n_cols=2 * C_HALF)
        proj_g = shifted(xn, w_in_bf, i, col0=OFF_GATES, n_cols=N_BRANCH * D_MODEL)
        alog_row = jnp.zeros((1, LANES), F32).at[0, A_HEADS:2 * A_HEADS].set(a_log[i])
        dtb_row = jnp.zeros((1, LANES), F32).at[0, A_HEADS:2 * A_HEADS].set(a_dt_bias[i])
        ya_p, s_p = _delta_prompt(proj_a, proj_ba, a_conv_w[i], alog_row, dtb_row, a_out_norm[i].reshape(1, A_DV),
                                  n_seq=nb, seq_len=ls, n_heads=A_HEADS, chunk=math.gcd(ls, DELTA_CHUNK))
        abuf_p = proj_a[:tp, :A_CONV_CH].reshape(nb, ls, A_CONV_CH)[:, ls - (A_CONV - 1):]
        ya_s, s_s = _delta_sample(proj_a, proj_ba, _seam_rows(state_a_conv[i], lq), a_conv_w[i], alog_row, dtb_row,
                                  a_out_norm[i].reshape(1, A_DV), state_delta[i],
                                  row0=tp, n_seq=ns, seq=lq, n_heads=A_HEADS)
        abuf_s = proj_a[tp:, :A_CONV_CH].reshape(ns, lq, A_CONV_CH)[:, lq - (A_CONV - 1):]
        yb_p = _attn_prompt(proj_q, proj_kv, b_sinks[i], n_seq=nb, seq_len=ls)
        wb = min(WINDOW, ls)
        kv_tail = proj_kv[:tp].reshape(nb, ls, 2, B_KV_HEADS, B_HD)[:, ls - wb:]
        k_p, v_p = kv_tail[:, :, 0], kv_tail[:, :, 1]
        wbs = cache_win_k.shape[2]
        yb_s = _attn_sample(proj_q, proj_kv, cache_win_k[i].reshape(ns, wbs, B_KV), cache_win_v[i].reshape(ns, wbs, B_KV),
                            b_sinks[i], row0=tp, n_seq=ns, seq=lq)
        kv_s = proj_kv[tp:].reshape(ns, lq, 2, B_KV_HEADS, B_HD)
        k_s = jnp.concatenate([cache_win_k[i], kv_s[:, :, 0]], axis=1)[:, lq:]
        v_s = jnp.concatenate([cache_win_v[i], kv_s[:, :, 1]], axis=1)[:, lq:]
        ln_g, ln_b = c_ln_g[i].reshape(1, C_HALF), c_ln_b[i].reshape(1, C_HALF)
        yc_p = _cmlp(proj_uv, ln_g, ln_b, c_w_s[i], c_b_s[i].T, row0=0, n_rows=tp, chunk=C_CHUNK, want_vn=False)
        eye_s = jnp.eye(ns, dtype=F32)
        w_s_blk = jax.vmap(lambda w: jnp.kron(eye_s, w))(c_w_s[i][:, :lq, :lq])
        yc_s, cv_s = _cmlp(proj_uv, ln_g, ln_b, w_s_blk, jnp.tile(c_b_s[i].T[:lq], (ns, 1)),
                           row0=tp, n_rows=ts, chunk=ts, want_vn=True)
        cv_s = cv_s.reshape(ns, lq, C_HALF)
        y_a = jnp.concatenate([ya_p, ya_s])
        y_b = jnp.concatenate([yb_p, yb_s.astype(BF16)])
        y_c = jnp.concatenate([yc_p, yc_s])
        merged = _gated_merge(y_a, y_b, y_c, w_br_a, w_br_b, w_br_c, i, proj_g, bm=BM, bn=BN)
        h, xf = _residual_norm(h, _matmul(merged, w_o, i, bm=BM, bn=BN), norm_mix_post[i], norm_ffn_pre[i])
        act, tail_g, tail_u, hs_g, hs_u = _ffn_up(
            xf, w_up, i, ffn_conv_w[i], ffn_conv_b[i].reshape(1, 2 * D_FF), _seam_rows(state_ffn_conv[i], lq),
            n_seq=nb, seq_len=ls, sample_rows=ts, sample_len=lq, bm=FFN_BM, bn=FFN_BN)
        tails = jnp.concatenate([tail_g, tail_u], axis=1).reshape(nb, ls // FFN_BM, 8, 2 * D_FF)
        fb_p = tails[:, -1, 8 - (FFN_CONV - 1):]
        fb_s = jnp.concatenate([hs_g, hs_u], axis=1).reshape(ns, lq, 2 * D_FF)[:, lq - (FFN_CONV - 1):]
        f = None
        for c in range(D_FF // D_MODEL):
            f = _matmul(act, w_down, i, bm=BM, bn=BN, k_chunk=c, bk=D_MODEL, acc=f)
        h, hn = _residual_norm(h, f, norm_ffn_post[i], norm_ple[i])
        p_i = jnp.concatenate([p_prompt[i].reshape(tp, ple), p_sample[i].reshape(ts, ple)]).astype(BF16)
        h = _ple_update(hn, p_i, w_ple_gate, w_ple, i, h, bm=BM, bn=BN)
        st_p.append((abuf_p, s_p, k_p, v_p, fb_p))
        st_s.append((abuf_s, s_s, k_s, v_s, fb_s, cv_s))

    def stack(states, j):
        return jnp.stack([s[j] for s in states])

    y_prompt = h[:tp].reshape(nb, ls, D_MODEL)
    y_sample = h[tp:].reshape(ns, lq, D_MODEL)
    return (y_prompt, y_sample,
            stack(st_p, 0), stack(st_p, 1), stack(st_p, 2), stack(st_p, 3), stack(st_p, 4),
            stack(st_s, 0), stack(st_s, 1), stack(st_s, 2), stack(st_s, 3), stack(st_s, 4),
            stack(st_s, 5))
```

```python
import functools
import math

import jax
import jax.numpy as jnp
from jax import lax
from jax.experimental import pallas as pl
from jax.experimental.pallas import tpu as pltpu

D_MODEL = 4096
DEPTH = 4
A_HEADS = 8
A_DK = 128
A_DV = 128
A_QK = A_HEADS * A_DK
A_VAL = A_HEADS * A_DV
A_CONV = 4
A_CONV_CH = 2 * A_QK + A_VAL
DELTA_CHUNK = 64
B_HEADS = 16
B_KV_HEADS = 4
B_GROUP = B_HEADS // B_KV_HEADS
B_HD = 64
B_Q = B_HEADS * B_HD
B_KV = B_KV_HEADS * B_HD
WINDOW = 128
C_GROUPS = 8
C_GC = 128
C_HALF = C_GROUPS * C_GC
C_CHUNK = 128
D_FF = 3 * D_MODEL
FFN_CONV = 3
N_BRANCH = 3
EPS = 1e-6

F32 = jnp.float32
BF16 = jnp.bfloat16
LANES = 128

VMEM_LIMIT_BYTES = 56 * 1024 * 1024

DELTA_RT = 256
DELTA_HP = 8
NEUMANN_BLOCK = 16
SAMPLE_LOCKSTEP = 4
NORM_ROWS = 264

BM = 1056
BN = 512
FFN_BM = 1024
FFN_BN = 256
OFF_BA = A_CONV_CH + A_VAL
SHIFT = 2 * A_HEADS
OFF_BQ = OFF_BA
OFF_KV = OFF_BQ + B_Q
OFF_CUV = OFF_KV + 2 * B_KV
OFF_GATES = OFF_CUV + 2 * C_HALF
BN_WIDE = 1024
CAST_ROWS = 512


def _params(n_axes):
    return pltpu.CompilerParams(dimension_semantics=("arbitrary",) * n_axes, vmem_limit_bytes=VMEM_LIMIT_BYTES)


def _sigmoid(x):
    return 1.0 / (1.0 + jnp.exp(-x))


def _softplus(x):
    return jnp.maximum(x, 0.0) + jnp.log(1.0 + jnp.exp(-jnp.abs(x)))


def _gelu_tanh(x):
    return 0.5 * x * (1.0 + jnp.tanh(0.7978845608028654 * (x + 0.044715 * (x * x * x))))


def _rms(x, g):
    return x * lax.rsqrt(jnp.mean(x * x, axis=-1, keepdims=True) + EPS) * g


def _dot(a, b):
    return jnp.dot(a, b, preferred_element_type=F32)


def _dot_nt(a, b):
    return lax.dot_general(a, b, (((1,), (1,)), ((), ())), preferred_element_type=F32)


def _dot_tn(a, b):
    return lax.dot_general(a, b, (((0,), (0,)), ((), ())), preferred_element_type=F32)


def _split2(x):
    hi = x.astype(BF16)
    lo = (x - hi.astype(F32)).astype(BF16)
    return hi, lo


def _dot3s(a, b):
    ah, al = a
    bh, bl = b
    return _dot(ah, bh) + (_dot(ah, bl) + _dot(al, bh))


def _mm_kernel(x_ref, w_ref, o_ref, wbf_ref):
    @pl.when(pl.program_id(1) == 0)
    def _():
        wbf_ref[...] = w_ref[...].astype(BF16)

    o_ref[...] = _dot(x_ref[...], wbf_ref[...])


def _mm_acc_kernel(x_ref, w_ref, a_ref, o_ref, wbf_ref):
    @pl.when(pl.program_id(1) == 0)
    def _():
        wbf_ref[...] = w_ref[...].astype(BF16)

    o_ref[...] = a_ref[...] + _dot(x_ref[...], wbf_ref[...])


def _mm_bf16_kernel(x_ref, w_ref, o_ref):
    o_ref[...] = _dot(x_ref[...], w_ref[...])


def _matmul(x, w, layer, *, bm, bn, n_cols=None, col0=0, k_chunk=0, bk=None, acc=None):
    m = x.shape[0]
    bk = w.shape[1] if bk is None else bk
    n_cols = w.shape[2] if n_cols is None else n_cols
    assert m % bm == 0 and n_cols % bn == 0 and col0 % bn == 0
    cb0 = col0 // bn
    grid = (n_cols // bn, m // bm)
    x_spec = pl.BlockSpec((bm, bk), lambda j, i: (i, k_chunk))
    w_spec = pl.BlockSpec((None, bk, bn), lambda j, i: (layer, k_chunk, cb0 + j))
    o_spec = pl.BlockSpec((bm, bn), lambda j, i: (i, j))
    out_shape = jax.ShapeDtypeStruct((m, n_cols), F32)
    if w.dtype == BF16:
        assert acc is None
        return pl.pallas_call(_mm_bf16_kernel, grid=grid, in_specs=[x_spec, w_spec], out_specs=o_spec,
                              out_shape=out_shape, compiler_params=_params(2), name="proj_bf16")(x, w)
    scratch = [pltpu.VMEM((bk, bn), BF16)]
    if acc is None:
        return pl.pallas_call(_mm_kernel, grid=grid, in_specs=[x_spec, w_spec], out_specs=o_spec,
                              out_shape=out_shape, scratch_shapes=scratch, compiler_params=_params(2),
                              name="proj")(x, w)
    return pl.pallas_call(_mm_acc_kernel, grid=grid, in_specs=[x_spec, w_spec, o_spec], out_specs=o_spec,
                          out_shape=out_shape, scratch_shapes=scratch, compiler_params=_params(2),
                          input_output_aliases={2: 0}, name="proj_acc")(x, w, acc)


def _mm_shift_kernel(x_ref, wa_ref, wb_ref, o_ref, wsh_ref, *, shift):
    @pl.when(pl.program_id(1) == 0)
    def _():
        k, bn = wa_ref.shape
        for r0 in range(0, k, CAST_ROWS):
            rs = slice(r0, r0 + CAST_ROWS)
            w = jnp.concatenate([wa_ref[rs, :], wb_ref[rs, :]], axis=1).astype(F32)
            wsh_ref[rs, :] = pltpu.roll(w, w.shape[1] - shift, axis=1)[:, :bn].astype(BF16)

    o_ref[...] = _dot(x_ref[...], wsh_ref[...])


def _matmul_shifted(x, w, layer, *, bm, bn, col0, n_cols, shift):
    m, k = x.shape
    assert m % bm == 0 and n_cols % bn == 0 and col0 % bn == 0 and 0 < shift < LANES and k % CAST_ROWS == 0
    assert w.dtype == BF16
    cb0 = col0 // bn
    lpb = bn // LANES
    return pl.pallas_call(
        functools.partial(_mm_shift_kernel, shift=shift), grid=(n_cols // bn, m // bm),
        in_specs=[pl.BlockSpec((bm, k), lambda j, i: (i, 0)),
                  pl.BlockSpec((None, k, bn), lambda j, i: (layer, 0, cb0 + j)),
                  pl.BlockSpec((None, k, LANES), lambda j, i: (layer, 0, (cb0 + j + 1) * lpb))],
        out_specs=pl.BlockSpec((bm, bn), lambda j, i: (i, j)),
        out_shape=jax.ShapeDtypeStruct((m, n_cols), F32),
        scratch_shapes=[pltpu.VMEM((k, bn), BF16)],
        compiler_params=_params(2), name="proj_shift",
    )(x, w, w)


def _norm_kernel(h_ref, g_ref, xn_ref):
    xn_ref[...] = _rms(h_ref[...], g_ref[...]).astype(xn_ref.dtype)


def _resnorm_kernel(h_ref, y_ref, gp_ref, gn_ref, ho_ref, xn_ref):
    h = h_ref[...] + _rms(y_ref[...], gp_ref[...])
    ho_ref[...] = h
    xn_ref[...] = _rms(h, gn_ref[...]).astype(xn_ref.dtype)


def _rms_norm_bf16(h, g):
    t, d = h.shape
    assert t % NORM_ROWS == 0
    row = pl.BlockSpec((NORM_ROWS, d), lambda r: (r, 0))
    vec = pl.BlockSpec((1, d), lambda r: (0, 0))
    return pl.pallas_call(_norm_kernel, grid=(t // NORM_ROWS,), in_specs=[row, vec], out_specs=row,
                          out_shape=jax.ShapeDtypeStruct((t, d), BF16), compiler_params=_params(1),
                          name="rms_norm")(h, g.reshape(1, d))


def _residual_norm(h, y, g_post, g_next):
    t, d = h.shape
    assert t % NORM_ROWS == 0
    row = pl.BlockSpec((NORM_ROWS, d), lambda r: (r, 0))
    vec = pl.BlockSpec((1, d), lambda r: (0, 0))
    return pl.pallas_call(
        _resnorm_kernel, grid=(t // NORM_ROWS,), in_specs=[row, row, vec, vec], out_specs=[row, row],
        out_shape=[jax.ShapeDtypeStruct((t, d), F32), jax.ShapeDtypeStruct((t, d), BF16)],
        input_output_aliases={0: 0}, compiler_params=_params(1), name="residual_norm",
    )(h, y, g_post.reshape(1, d), g_next.reshape(1, d))


def _run(gen):
    try:
        while True:
            next(gen)
    except StopIteration as stop:
        return stop.value


def _lockstep(gens):
    results = [None] * len(gens)
    live = list(range(len(gens)))
    while live:
        still = []
        for idx in live:
            try:
                next(gens[idx])
                still.append(idx)
            except StopIteration as stop:
                results[idx] = stop.value
        live = still
    return results


def _unit_lower_inverse(a, rx, c):
    eye = (rx == 0).astype(F32)
    blk = min(NEUMANN_BLOCK, c)
    n = -jnp.where(rx < blk, a, 0.0)
    p = eye + n
    pw = n
    for _ in range(int(math.log2(blk)) - 1):
        pws = _split2(pw)
        pw = _dot3s(pws, pws)
        yield
        p = p + _dot3s(_split2(p), _split2(pw))
        yield
    while blk < c:
        e = jnp.where((rx >= blk) & (rx < 2 * blk), a, 0.0)
        ps = _split2(p)
        ep = _dot3s(_split2(e), ps)
        yield
        p = p - _dot3s(ps, _split2(ep))
        yield
        blk *= 2
    return p


def _shift_rows(x, prev, j, pos):
    rp = pltpu.roll(prev, j, axis=0)
    if prev.shape[0] != x.shape[0]:
        rp = jnp.broadcast_to(rp[None], (x.shape[0] // 8, 8, x.shape[1])).reshape(x.shape)
    return jnp.where(pos < j, rp, pltpu.roll(x, j, axis=0))


def _conv_silu(x, xprev, cw, seg):
    pos = lax.broadcasted_iota(jnp.int32, x.shape, 0) & (seg - 1)
    acc = x * cw[3:4, :]
    for j in range(1, 4):
        acc = acc + _shift_rows(x, xprev, j, pos) * cw[3 - j:4 - j, :]
    return acc * _sigmoid(acc)


def _delta_tile_prep(q_raw, k_raw, v_raw, pq, pk, pv, cwq, cwk, cwv, beta_col, g_col, c, seg):
    rt = q_raw.shape[0]
    q = _conv_silu(q_raw, pq, cwq, seg)
    k = _conv_silu(k_raw, pk, cwk, seg)
    v = _conv_silu(v_raw, pv, cwv, seg)
    q = q * (lax.rsqrt(jnp.sum(q * q, axis=-1, keepdims=True) + EPS) * (A_DK ** -0.5))
    k = k * lax.rsqrt(jnp.sum(k * k, axis=-1, keepdims=True) + EPS)
    yield

    r_i = lax.broadcasted_iota(jnp.int32, (rt, rt), 0)
    c_i = lax.broadcasted_iota(jnp.int32, (rt, rt), 1)
    rx = r_i ^ c_i
    same = rx < c
    lower = same & (c_i <= r_i)
    strict = same & (c_i < r_i)

    g1 = g_col.astype(BF16)
    r1 = g_col - g1.astype(F32)
    g2 = r1.astype(BF16)
    g3 = (r1 - g2.astype(F32)).astype(BF16)
    lmat = lower.astype(BF16)
    umat = (same & (r_i <= c_i)).astype(BF16)
    ones = jnp.ones((rt, rt), BF16)
    gc = None
    gr = None
    for gp in (g1, g2, g3):
        a = _dot(lmat, jnp.broadcast_to(gp, (rt, LANES)))
        b = _dot(ones, jnp.broadcast_to(gp, (rt, rt)) * umat)
        gc = a if gc is None else gc + a
        gr = b if gr is None else gr + b
    gc_full = jnp.concatenate([gc] * (rt // LANES), axis=1)
    decay = jnp.where(lower, jnp.exp(jnp.where(lower, gc_full - gr, 0.0)), 0.0)
    yield

    kb = k.astype(BF16)
    kk = _dot_nt(kb, kb)
    a_mat = jnp.where(strict, beta_col * kk * decay, 0.0)
    yield
    t_inv = yield from _unit_lower_inverse(a_mat, rx, c)
    rhs = jnp.concatenate([beta_col * v, (beta_col * jnp.exp(gc)) * k], axis=1)
    sol = _dot3s(_split2(t_inv), _split2(rhs))
    yield
    u_base = sol[:, :LANES]
    w_mat = sol[:, LANES:]
    qk = jnp.where(lower, _dot_nt(q.astype(BF16), kb) * decay, 0.0)
    q_dec = q * jnp.exp(gc)
    yield
    return k, gc, u_base, w_mat, qk, q_dec


def _gated_out_norm(o, z, og):
    return _rms(o, og) * (z * _sigmoid(z))


def _head_cols(ba, alog_row, dtb_row, h, n_heads):
    lane = lax.broadcasted_iota(jnp.int32, ba.shape, 1)
    sig = _sigmoid(ba)
    g_all = -jnp.exp(alog_row) * _softplus(ba + dtb_row)
    beta_col = jnp.sum(jnp.where(lane == h, sig, 0.0), axis=1, keepdims=True)
    g_col = jnp.sum(jnp.where(lane == n_heads + h, g_all, 0.0), axis=1, keepdims=True)
    return beta_col, g_col


def _delta_prompt_kernel(q_ref, k_ref, v_ref, z_ref, ba_ref, cwq_ref, cwk_ref, cwv_ref, alog_ref, dtb_ref, og_ref,
                         y_ref, sfin_ref, xprev_ref, s_ref, *, n_heads, chunk):
    j = pl.program_id(2)
    hp = pl.program_id(1)
    rt = q_ref.shape[0]

    @pl.when(j == 0)
    def _():
        xprev_ref[...] = jnp.zeros_like(xprev_ref)
        s_ref[...] = jnp.zeros_like(s_ref)

    ba = ba_ref[...]

    def head(hh):
        cs = slice(hh * LANES, (hh + 1) * LANES)
        beta_col, g_col = _head_cols(ba, alog_ref[...], dtb_ref[...], hp * DELTA_HP + hh, n_heads)
        q_raw, k_raw, v_raw = q_ref[:, cs], k_ref[:, cs], v_ref[:, cs]
        k, gc, u_base, w_mat, qk, q_dec = yield from _delta_tile_prep(
            q_raw, k_raw, v_raw, xprev_ref[0, :, cs], xprev_ref[1, :, cs], xprev_ref[2, :, cs],
            cwq_ref[:, cs], cwk_ref[:, cs], cwv_ref[:, cs], beta_col, g_col, chunk, rt)
        xprev_ref[0, :, cs] = q_raw[rt - 8:]
        xprev_ref[1, :, cs] = k_raw[rt - 8:]
        xprev_ref[2, :, cs] = v_raw[rt - 8:]
        s = s_ref[hh]
        us, os_ = [], []
        for cc in range(rt // chunk):
            rs = slice(cc * chunk, (cc + 1) * chunk)
            sb = s.astype(BF16)
            u_c = u_base[rs] - _dot(w_mat[rs].astype(BF16), sb)
            os_.append(_dot(q_dec[rs].astype(BF16), sb))
            yield
            gl = gc[(cc + 1) * chunk - 1:(cc + 1) * chunk, :]
            k_dec = k[rs] * jnp.exp(gl - gc[rs])
            s = jnp.exp(gl) * s + _dot_tn(k_dec.astype(BF16), u_c.astype(BF16))
            us.append(u_c)
            yield
        s_ref[hh] = s
        u_all = jnp.concatenate(us, axis=0)
        o = jnp.concatenate(os_, axis=0) + _dot(qk.astype(BF16), u_all.astype(BF16))
        yield
        y_ref[:, cs] = _gated_out_norm(o, z_ref[:, cs], og_ref[...]).astype(y_ref.dtype)

    _lockstep([head(hh) for hh in range(DELTA_HP)])

    @pl.when(j == pl.num_programs(2) - 1)
    def _():
        sfin_ref[0] = s_ref[...]


def _delta_prompt(proj_a, proj_ba, conv_w, alog_row, dtb_row, out_g, *, n_seq, seq_len, n_heads, chunk):
    rt, hp = DELTA_RT, DELTA_HP
    assert seq_len % rt == 0 and rt % chunk == 0 and n_heads % hp == 0
    tiles = seq_len // rt
    nhp = n_heads // hp
    wblk = hp * LANES

    def col_spec(base):
        return pl.BlockSpec((rt, wblk), lambda n, h, j: (n * tiles + j, base + h))

    def cw_spec(base):
        return pl.BlockSpec((4, wblk), lambda n, h, j: (0, base + h))

    row128 = pl.BlockSpec((1, LANES), lambda n, h, j: (0, 0))
    kern = functools.partial(_delta_prompt_kernel, n_heads=n_heads, chunk=chunk)
    return pl.pallas_call(
        kern, grid=(n_seq, nhp, tiles),
        in_specs=[col_spec(0), col_spec(nhp), col_spec(2 * nhp), col_spec(3 * nhp),
                  pl.BlockSpec((rt, LANES), lambda n, h, j: (n * tiles + j, 0)),
                  cw_spec(0), cw_spec(nhp), cw_spec(2 * nhp), row128, row128, row128],
        out_specs=[pl.BlockSpec((rt, wblk), lambda n, h, j: (n * tiles + j, h)),
                   pl.BlockSpec((1, hp, LANES, LANES), lambda n, h, j: (n, h, 0, 0))],
        out_shape=[jax.ShapeDtypeStruct((n_seq * seq_len, n_heads * LANES), BF16),
                   jax.ShapeDtypeStruct((n_seq, n_heads, LANES, LANES), F32)],
        scratch_shapes=[pltpu.VMEM((3, 8, wblk), F32), pltpu.VMEM((hp, LANES, LANES), F32)],
        compiler_params=_params(3), name="delta_prompt",
    )(proj_a, proj_a, proj_a, proj_a, proj_ba, conv_w, conv_w, conv_w, alog_row, dtb_row, out_g)


def _attn_prompt_kernel(q_ref, kc_ref, kp_ref, vc_ref, vp_ref, sk_ref, o_ref, *, blocks_per_seq):
    b = pl.program_id(0) % blocks_per_seq
    w = q_ref.shape[0]
    r_i = lax.broadcasted_iota(jnp.int32, (w, 2 * w), 0)
    c_i = lax.broadcasted_iota(jnp.int32, (w, 2 * w), 1)
    mask = (c_i > r_i) & (c_i <= r_i + w) & ((b > 0) | (c_i >= w))
    scale = B_HD ** -0.5
    for kvh in range(B_KV_HEADS):
        ks = slice(kvh * B_HD, (kvh + 1) * B_HD)
        kk = jnp.concatenate([kp_ref[:, ks], kc_ref[:, ks]], axis=0).astype(BF16)
        vv = jnp.concatenate([vp_ref[:, ks], vc_ref[:, ks]], axis=0).astype(BF16)
        for g in range(B_GROUP):
            h = kvh * B_GROUP + g
            hs = slice(h * B_HD, (h + 1) * B_HD)
            s = _dot_nt(q_ref[:, hs].astype(BF16), kk) * scale
            s = jnp.where(mask, s, -jnp.inf)
            sink = sk_ref[h]
            m = jnp.maximum(jnp.max(s, axis=-1, keepdims=True), sink)
            e = jnp.exp(s - m)
            p = e / (jnp.sum(e, axis=-1, keepdims=True) + jnp.exp(sink - m))
            o_ref[:, hs] = _dot(p.astype(BF16), vv).astype(o_ref.dtype)


def _attn_prompt(q_arr, kv_arr, sinks, *, n_seq, seq_len):
    w = WINDOW
    bps = seq_len // w
    cur = lambda r: (r, 0)
    return pl.pallas_call(
        functools.partial(_attn_prompt_kernel, blocks_per_seq=bps), grid=(n_seq * bps,),
        in_specs=[pl.BlockSpec((w, B_Q), cur),
                  pl.BlockSpec((w, B_KV), cur), pl.BlockSpec((w, B_KV), lambda r: (jnp.maximum(r - 1, 0), 0)),
                  pl.BlockSpec((w, B_KV), lambda r: (r, 1)), pl.BlockSpec((w, B_KV), lambda r: (jnp.maximum(r - 1, 0), 1)),
                  pl.BlockSpec(memory_space=pltpu.SMEM)],
        out_specs=pl.BlockSpec((w, B_Q), cur),
        out_shape=jax.ShapeDtypeStruct((n_seq * seq_len, B_Q), BF16),
        compiler_params=_params(1), name="attn_prompt",
    )(q_arr, kv_arr, kv_arr, kv_arr, kv_arr, sinks)


def _cmlp_kernel(u_ref, v_ref, lg_ref, lb_ref, ws_ref, bs_ref, o_ref, *vn_ref):
    v = _gelu_tanh(v_ref[...])
    mu = jnp.mean(v, axis=-1, keepdims=True)
    d = v - mu
    var = jnp.mean(d * d, axis=-1, keepdims=True)
    vn = d * lax.rsqrt(var + EPS) * lg_ref[...] + lb_ref[...]
    if vn_ref:
        vn_ref[0][...] = vn
    c = v.shape[0]
    tril = lax.broadcasted_iota(jnp.int32, (c, c), 1) <= lax.broadcasted_iota(jnp.int32, (c, c), 0)
    for g in range(C_GROUPS):
        gs = slice(g * C_GC, (g + 1) * C_GC)
        wc = jnp.where(tril, ws_ref[g], 0.0).astype(BF16)
        mixed = _dot(wc, vn[:, gs].astype(BF16)) + bs_ref[:, g:g + 1]
        o_ref[:, gs] = (_gelu_tanh(u_ref[:, gs]) * mixed).astype(o_ref.dtype)


def _cmlp(uv_arr, ln_g, ln_b, w_s, b_s_t, *, row0, n_rows, chunk, want_vn):
    assert row0 % chunk == 0 and n_rows % chunk == 0
    rb0 = row0 // chunk
    blk = pl.BlockSpec((chunk, C_HALF), lambda r: (r, 0))
    vec = pl.BlockSpec((1, C_HALF), lambda r: (0, 0))
    out_specs = [blk, blk] if want_vn else blk
    y_shape = jax.ShapeDtypeStruct((n_rows, C_HALF), BF16)
    out_shape = [y_shape, jax.ShapeDtypeStruct((n_rows, C_HALF), F32)] if want_vn else y_shape
    return pl.pallas_call(
        _cmlp_kernel, grid=(n_rows // chunk,),
        in_specs=[pl.BlockSpec((chunk, C_HALF), lambda r: (rb0 + r, 0)),
                  pl.BlockSpec((chunk, C_HALF), lambda r: (rb0 + r, 1)), vec, vec,
                  pl.BlockSpec((C_GROUPS, chunk, chunk), lambda r: (0, 0, 0)),
                  pl.BlockSpec((chunk, C_GROUPS), lambda r: (0, 0))],
        out_specs=out_specs, out_shape=out_shape, compiler_params=_params(1), name="cmlp",
    )(uv_arr, uv_arr, ln_g, ln_b, w_s, b_s_t)


def _delta_sample_kernel(q_ref, k_ref, v_ref, z_ref, ba_ref, pq_ref, pk_ref, pv_ref, cwq_ref, cwk_ref, cwv_ref,
                         alog_ref, dtb_ref, og_ref, s0_ref, y_ref, snew_ref,
                         ub_ref, wm_ref, qd_ref, kd_ref, eg_ref, u_ref, oi_ref, *, n_heads, seq):
    h = pl.program_id(0)
    rt = q_ref.shape[0]
    beta_col, g_col = _head_cols(ba_ref[...], alog_ref[...], dtb_ref[...], h, n_heads)
    k, gc, u_base, w_mat, qk, q_dec = _run(_delta_tile_prep(
        q_ref[...], k_ref[...], v_ref[...], pq_ref[...], pk_ref[...], pv_ref[...],
        cwq_ref[...], cwk_ref[...], cwv_ref[...], beta_col, g_col, seq, seq))
    r_i = lax.broadcasted_iota(jnp.int32, (rt, rt), 0)
    c_i = lax.broadcasted_iota(jnp.int32, (rt, rt), 1)
    sel = (c_i == (r_i | (seq - 1))).astype(BF16)
    g1 = gc.astype(BF16)
    r1 = gc - g1.astype(F32)
    g2 = r1.astype(BF16)
    g3 = (r1 - g2.astype(F32)).astype(BF16)
    gl = _dot(sel, g1) + (_dot(sel, g2) + _dot(sel, g3))
    pad = jnp.zeros((8, LANES), F32)
    for ref, val in ((ub_ref, u_base), (wm_ref, w_mat), (qd_ref, q_dec), (kd_ref, k * jnp.exp(gl - gc)),
                     (eg_ref, jnp.exp(gl))):
        ref[0:rt, :] = val
        ref[rt:rt + 8, :] = pad

    def one_sequence(s):
        r0 = pl.multiple_of(s * seq, seq)
        win = pl.ds(r0, 2 * seq)
        s0 = s0_ref[s]
        sb = s0.astype(BF16)
        first = lax.broadcasted_iota(jnp.int32, (2 * seq, LANES), 0) < seq
        u16 = ub_ref[win, :] - _dot(wm_ref[win, :].astype(BF16), sb)
        o16 = _dot(qd_ref[win, :].astype(BF16), sb)
        yield
        kd16 = jnp.where(first, kd_ref[win, :], 0.0)
        snew_ref[s] = eg_ref[pl.ds(r0, 1), :] * s0 + _dot_tn(kd16.astype(BF16), jnp.where(first, u16, 0.0).astype(BF16))
        u_ref[pl.ds(r0, seq), :] = u16[0:seq]
        oi_ref[pl.ds(r0, seq), :] = o16[0:seq]

    def body(i, carry):
        _lockstep([one_sequence(SAMPLE_LOCKSTEP * i + d) for d in range(SAMPLE_LOCKSTEP)])
        return carry

    lax.fori_loop(0, rt // seq // SAMPLE_LOCKSTEP, body, 0)
    o = oi_ref[...] + _dot(qk.astype(BF16), u_ref[...].astype(BF16))
    y_ref[...] = _gated_out_norm(o, z_ref[...], og_ref[...]).astype(y_ref.dtype)


def _delta_sample(proj_a, proj_ba, p_conv, conv_w, alog_row, dtb_row, out_g, s0, *, row0, n_seq, seq, n_heads):
    rt = n_seq * seq
    assert seq == 8 and row0 % rt == 0 and rt % LANES == 0
    rb = row0 // rt

    def col_spec(base):
        return pl.BlockSpec((rt, LANES), lambda h: (rb, base + h))

    def p_spec(base):
        return pl.BlockSpec((rt, LANES), lambda h: (0, base + h))

    def cw_spec(base):
        return pl.BlockSpec((4, LANES), lambda h: (0, base + h))

    row128 = pl.BlockSpec((1, LANES), lambda h: (0, 0))
    st_spec = pl.BlockSpec((n_seq, None, LANES, LANES), lambda h: (0, h, 0, 0))
    big = pltpu.VMEM((rt + 8, LANES), F32)
    kern = functools.partial(_delta_sample_kernel, n_heads=n_heads, seq=seq)
    return pl.pallas_call(
        kern, grid=(n_heads,),
        in_specs=[col_spec(0), col_spec(n_heads), col_spec(2 * n_heads), col_spec(3 * n_heads),
                  pl.BlockSpec((rt, LANES), lambda h: (rb, 0)),
                  p_spec(0), p_spec(n_heads), p_spec(2 * n_heads),
                  cw_spec(0), cw_spec(n_heads), cw_spec(2 * n_heads), row128, row128, row128, st_spec],
        out_specs=[pl.BlockSpec((rt, LANES), lambda h: (0, h)), st_spec],
        out_shape=[jax.ShapeDtypeStruct((rt, n_heads * LANES), BF16),
                   jax.ShapeDtypeStruct((n_seq, n_heads, LANES, LANES), F32)],
        scratch_shapes=[big, big, big, big, big, pltpu.VMEM((rt, LANES), F32), pltpu.VMEM((rt, LANES), F32)],
        compiler_params=_params(1), name="delta_sample",
    )(proj_a, proj_a, proj_a, proj_a, proj_ba, p_conv, p_conv, p_conv, conv_w, conv_w, conv_w,
      alog_row, dtb_row, out_g, s0)


def _attn_sample_kernel(q_ref, kvn_ref, kb_ref, vb_ref, sk_ref, o_ref):
    lq = q_ref.shape[0]
    wb = kb_ref.shape[0]
    nk = 2 * wb
    rows = B_GROUP * lq
    r_i = lax.broadcasted_iota(jnp.int32, (rows, nk), 0)
    c_i = lax.broadcasted_iota(jnp.int32, (rows, nk), 1)
    i = r_i & (lq - 1)
    mask = (c_i <= i + wb) & (c_i > i + wb - WINDOW)
    grp = lax.shift_right_logical(lax.broadcasted_iota(jnp.int32, (rows, 1), 0), int(math.log2(lq)))
    scale = B_HD ** -0.5
    zpad = jnp.zeros((nk - wb - lq, B_HD), F32)
    def kv_head(kvh):
        ks = slice(kvh * B_HD, (kvh + 1) * B_HD)
        vs = slice(B_KV + kvh * B_HD, B_KV + (kvh + 1) * B_HD)
        kk = jnp.concatenate([kb_ref[:, ks], kvn_ref[:, ks], zpad], axis=0).astype(BF16)
        vv = jnp.concatenate([vb_ref[:, ks], kvn_ref[:, vs], zpad], axis=0).astype(BF16)
        h0 = kvh * B_GROUP
        q = jnp.concatenate([q_ref[:, (h0 + g) * B_HD:(h0 + g + 1) * B_HD] for g in range(B_GROUP)], axis=0)
        s = jnp.where(mask, _dot_nt(q.astype(BF16), kk) * scale, -jnp.inf)
        yield
        sink = jnp.zeros((rows, 1), F32)
        for g in range(B_GROUP):
            sink = jnp.where(grp == g, sk_ref[h0 + g], sink)
        m = jnp.maximum(jnp.max(s, axis=-1, keepdims=True), sink)
        e = jnp.exp(s - m)
        p = e / (jnp.sum(e, axis=-1, keepdims=True) + jnp.exp(sink - m))
        yield
        o = _dot(p.astype(BF16), vv)
        yield
        for g in range(B_GROUP):
            o_ref[:, (h0 + g) * B_HD:(h0 + g + 1) * B_HD] = o[g * lq:(g + 1) * lq]

    _lockstep([kv_head(kvh) for kvh in range(B_KV_HEADS)])


def _attn_sample(q_arr, kv_arr, k_buf, v_buf, sinks, *, row0, n_seq, seq):
    assert row0 % seq == 0 and seq & (seq - 1) == 0 and k_buf.shape[1] >= seq
    rb0 = row0 // seq
    wb = k_buf.shape[1]
    buf = pl.BlockSpec((None, wb, B_KV), lambda s: (s, 0, 0))
    return pl.pallas_call(
        _attn_sample_kernel, grid=(n_seq,),
        in_specs=[pl.BlockSpec((seq, B_Q), lambda s: (rb0 + s, 0)), pl.BlockSpec((seq, 2 * B_KV), lambda s: (rb0 + s, 0)),
                  buf, buf, pl.BlockSpec(memory_space=pltpu.SMEM)],
        out_specs=pl.BlockSpec((seq, B_Q), lambda s: (s, 0)),
        out_shape=jax.ShapeDtypeStruct((n_seq * seq, B_Q), F32),
        compiler_params=_params(1), name="attn_sample",
    )(q_arr, kv_arr, k_buf, v_buf, sinks)


def _merge_kernel(ya_ref, yb_ref, yc_ref, wa_ref, wb_ref, wc_ref, ga_ref, gb_ref, gc_ref, o_ref,
                  wab_ref, wbb_ref, wcb_ref):
    @pl.when(pl.program_id(1) == 0)
    def _():
        wab_ref[...] = wa_ref[...].astype(BF16)
        wbb_ref[...] = wb_ref[...].astype(BF16)
        wcb_ref[...] = wc_ref[...].astype(BF16)

    acc = _sigmoid(ga_ref[...]) * _dot(ya_ref[...], wab_ref[...])
    acc = acc + _sigmoid(gb_ref[...]) * _dot(yb_ref[...], wbb_ref[...])
    acc = acc + _sigmoid(gc_ref[...]) * _dot(yc_ref[...], wcb_ref[...])
    o_ref[...] = acc.astype(o_ref.dtype)


def _gated_merge(y_a, y_b, y_c, w_a, w_b, w_c, layer, gate_logits, *, bm, bn):
    t, kdim = y_a.shape
    d = w_a.shape[2]
    assert t % bm == 0 and d % bn == 0
    nj = d // bn
    y_spec = pl.BlockSpec((bm, kdim), lambda j, i: (i, 0))
    w_spec = pl.BlockSpec((None, kdim, bn), lambda j, i: (layer, 0, j))

    def g_spec(b):
        return pl.BlockSpec((bm, bn), lambda j, i: (i, b * nj + j))

    return pl.pallas_call(
        _merge_kernel, grid=(nj, t // bm),
        in_specs=[y_spec, y_spec, y_spec, w_spec, w_spec, w_spec, g_spec(0), g_spec(1), g_spec(2)],
        out_specs=pl.BlockSpec((bm, bn), lambda j, i: (i, j)),
        out_shape=jax.ShapeDtypeStruct((t, d), BF16),
        scratch_shapes=[pltpu.VMEM((kdim, bn), BF16)] * 3,
        compiler_params=_params(2), name="gated_merge",
    )(y_a, y_b, y_c, w_a, w_b, w_c, gate_logits, gate_logits, gate_logits)


def _conv3(h, hprev, cw, b, pos):
    acc = h * cw[2:3, :] + b
    for j in (1, 2):
        acc = acc + _shift_rows(h, hprev, j, pos) * cw[2 - j:3 - j, :]
    return acc


def _ffn_up_kernel(x_ref, wg_ref, wu_ref, cwg_ref, cwu_ref, bg_ref, bu_ref, pg_ref, pu_ref,
                   act_ref, tg_ref, tu_ref, hsg_ref, hsu_ref, wgb_ref, wub_ref, prevg_ref, prevu_ref,
                   *, tiles_per_seq, n_prompt_tiles, sample_rows, sample_len):
    m = pl.program_id(1)
    bm = x_ref.shape[0]

    @pl.when(m == 0)
    def _():
        wgb_ref[...] = wg_ref[...].astype(BF16)
        wub_ref[...] = wu_ref[...].astype(BF16)

    @pl.when(m < n_prompt_tiles)
    def _():
        x = x_ref[...]
        hg = _dot(x, wgb_ref[...])
        hu = _dot(x, wub_ref[...])
        first = (m % tiles_per_seq) == 0
        pos = lax.broadcasted_iota(jnp.int32, hg.shape, 0)
        pg = jnp.where(first, 0.0, prevg_ref[...])
        pu = jnp.where(first, 0.0, prevu_ref[...])
        g = _conv3(hg, pg, cwg_ref[...], bg_ref[...], pos)
        u = _conv3(hu, pu, cwu_ref[...], bu_ref[...], pos)
        act_ref[...] = (_gelu_tanh(g) * u).astype(act_ref.dtype)
        prevg_ref[...] = hg
        prevu_ref[...] = hu
        tg_ref[...] = hg[bm - 8:, :]
        tu_ref[...] = hu[bm - 8:, :]

    @pl.when(m == n_prompt_tiles)
    def _():
        x = x_ref[0:sample_rows, :]
        hg = _dot(x, wgb_ref[...])
        hu = _dot(x, wub_ref[...])
        pos = lax.broadcasted_iota(jnp.int32, hg.shape, 0) & (sample_len - 1)
        g = _conv3(hg, pg_ref[...], cwg_ref[...], bg_ref[...], pos)
        u = _conv3(hu, pu_ref[...], cwu_ref[...], bu_ref[...], pos)
        act_ref[0:sample_rows, :] = (_gelu_tanh(g) * u).astype(act_ref.dtype)
        hsg_ref[...] = hg
        hsu_ref[...] = hu


def _ffn_up(x, w_up, layer, conv_w, conv_b, p_state, *, n_seq, seq_len, sample_rows, sample_len, bm, bn):
    t, d = x.shape
    f = w_up.shape[2] // 2
    tp = n_seq * seq_len
    assert seq_len % bm == 0 and f % bn == 0 and t == tp + sample_rows and sample_rows <= bm
    npt = tp // bm
    nj = f // bn
    kern = functools.partial(_ffn_up_kernel, tiles_per_seq=seq_len // bm, n_prompt_tiles=npt,
                             sample_rows=sample_rows, sample_len=sample_len)

    def half(rows, base):
        return pl.BlockSpec((rows, bn), lambda j, m: (0, base + j))

    def w_half(base):
        return pl.BlockSpec((None, d, bn), lambda j, m: (layer, 0, base + j))

    tail_spec = pl.BlockSpec((8, bn), lambda j, m: (jnp.minimum(m, npt - 1), j))
    return pl.pallas_call(
        kern, grid=(nj, npt + 1),
        in_specs=[pl.BlockSpec((bm, d), lambda j, m: (m, 0)),
                  w_half(0), w_half(nj), half(3, 0), half(3, nj), half(1, 0), half(1, nj),
                  half(sample_rows, 0), half(sample_rows, nj)],
        out_specs=[pl.BlockSpec((bm, bn), lambda j, m: (m, j)), tail_spec, tail_spec,
                   half(sample_rows, 0), half(sample_rows, 0)],
        out_shape=[jax.ShapeDtypeStruct((t, f), BF16),
                   jax.ShapeDtypeStruct((npt * 8, f), F32), jax.ShapeDtypeStruct((npt * 8, f), F32),
                   jax.ShapeDtypeStruct((sample_rows, f), F32), jax.ShapeDtypeStruct((sample_rows, f), F32)],
        scratch_shapes=[pltpu.VMEM((d, bn), BF16), pltpu.VMEM((d, bn), BF16),
                        pltpu.VMEM((bm, bn), F32), pltpu.VMEM((bm, bn), F32)],
        compiler_params=_params(2), name="ffn_up",
    )(x, w_up, w_up, conv_w, conv_w, conv_b, conv_b, p_state, p_state)


def _ple_kernel(x_ref, p_ref, wg_ref, wp_ref, h_ref, o_ref, wgb_ref, wpb_ref):
    @pl.when(pl.program_id(1) == 0)
    def _():
        wgb_ref[...] = wg_ref[...].astype(BF16)
        wpb_ref[...] = wp_ref[...].astype(BF16)

    gate = _sigmoid(_dot(x_ref[...], wgb_ref[...]))
    o_ref[...] = h_ref[...] + gate * _dot(p_ref[...], wpb_ref[...])


def _ple_update(hn, p, w_gate, w_ple, layer, h, *, bm, bn):
    t, d = h.shape
    kp = p.shape[1]
    assert t % bm == 0 and d % bn == 0
    blk = pl.BlockSpec((bm, bn), lambda j, i: (i, j))
    return pl.pallas_call(
        _ple_kernel, grid=(d // bn, t // bm),
        in_specs=[pl.BlockSpec((bm, d), lambda j, i: (i, 0)), pl.BlockSpec((bm, kp), lambda j, i: (i, 0)),
                  pl.BlockSpec((None, d, bn), lambda j, i: (layer, 0, j)),
                  pl.BlockSpec((None, kp, bn), lambda j, i: (layer, 0, j)), blk],
        out_specs=blk, out_shape=jax.ShapeDtypeStruct((t, d), F32),
        scratch_shapes=[pltpu.VMEM((d, bn), BF16), pltpu.VMEM((kp, bn), BF16)],
        input_output_aliases={4: 0}, compiler_params=_params(2), name="ple_update",
    )(hn, p, w_gate, w_ple, h)


def _seam_rows(state, seq):
    n_seq, w, c = state.shape
    padded = jnp.pad(state, ((0, 0), (seq - w, 0), (0, 0)))
    return jnp.roll(padded, -1, axis=0).reshape(n_seq * seq, c)
def kernel(x_prompt, x_sample, p_prompt, p_sample, state_a_conv, state_delta, cache_win_k, cache_win_v, state_ffn_conv, norm_mix_pre, norm_mix_post, norm_ffn_pre, norm_ffn_post, norm_ple, w_in, a_conv_w, a_log, a_dt_bias, a_out_norm, b_sinks, c_ln_g, c_ln_b, c_w_s, c_b_s, w_br_a, w_br_b, w_br_c, w_o, w_up, ffn_conv_w, ffn_conv_b, w_down, w_ple, w_ple_gate):
    nb, ls = x_prompt.shape[:2]
    ns, lq = x_sample.shape[:2]
    tp, ts = nb * ls, ns * lq
    ple = p_prompt.shape[-1]
    h = jnp.concatenate([x_prompt.reshape(tp, D_MODEL), x_sample.reshape(ts, D_MODEL)], axis=0)
    w_in_bf = w_in.astype(BF16)
    shifted = functools.partial(_matmul_shifted, bm=BM, bn=BN, shift=SHIFT)
    st_p, st_s = [], []
    for i in range(DEPTH):
        xn = _rms_norm_bf16(h, norm_mix_pre[i])
        proj_a = _matmul(xn, w_in_bf, i, bm=BM, bn=BN_WIDE, n_cols=OFF_BA)
        proj_ba = _matmul(xn, w_in_bf, i, bm=BM, bn=LANES, col0=OFF_BA, n_cols=LANES)
        proj_q = shifted(xn, w_in_bf, i, col0=OFF_BQ, n_cols=B_Q)
        proj_kv = shifted(xn, w_in_bf, i, col0=OFF_KV, n_cols=2 * B_KV)
        proj_uv = shifted(xn, w_in_bf, i, col0=OFF_CUV, n_cols=2 * C_HALF)
        proj_g = shifted(xn, w_in_bf, i, col0=OFF_GATES, n_cols=N_BRANCH * D_MODEL)
        alog_row = jnp.zeros((1, LANES), F32).at[0, A_HEADS:2 * A_HEADS].set(a_log[i])
        dtb_row = jnp.zeros((1, LANES), F32).at[0, A_HEADS:2 * A_HEADS].set(a_dt_bias[i])
        ya_p, s_p = _delta_prompt(proj_a, proj_ba, a_conv_w[i], alog_row, dtb_row, a_out_norm[i].reshape(1, A_DV),
                                  n_seq=nb, seq_len=ls, n_heads=A_HEADS, chunk=math.gcd(ls, DELTA_CHUNK))
        abuf_p = proj_a[:tp, :A_CONV_CH].reshape(nb, ls, A_CONV_CH)[:, ls - (A_CONV - 1):]
        ya_s, s_s = _delta_sample(proj_a, proj_ba, _seam_rows(state_a_conv[i], lq), a_conv_w[i], alog_row, dtb_row,
                                  a_out_norm[i].reshape(1, A_DV), state_delta[i],
                                  row0=tp, n_seq=ns, seq=lq, n_heads=A_HEADS)
        abuf_s = proj_a[tp:, :A_CONV_CH].reshape(ns, lq, A_CONV_CH)[:, lq - (A_CONV - 1):]
        yb_p = _attn_prompt(proj_q, proj_kv, b_sinks[i], n_seq=nb, seq_len=ls)
        wb = min(WINDOW, ls)
        kv_tail = proj_kv[:tp].reshape(nb, ls, 2, B_KV_HEADS, B_HD)[:, ls - wb:]
        k_p, v_p = kv_tail[:, :, 0], kv_tail[:, :, 1]
        wbs = cache_win_k.shape[2]
        yb_s = _attn_sample(proj_q, proj_kv, cache_win_k[i].reshape(ns, wbs, B_KV), cache_win_v[i].reshape(ns, wbs, B_KV),
                            b_sinks[i], row0=tp, n_seq=ns, seq=lq)
        kv_s = proj_kv[tp:].reshape(ns, lq, 2, B_KV_HEADS, B_HD)
        k_s = jnp.concatenate([cache_win_k[i], kv_s[:, :, 0]], axis=1)[:, lq:]
        v_s = jnp.concatenate([cache_win_v[i], kv_s[:, :, 1]], axis=1)[:, lq:]
        ln_g, ln_b = c_ln_g[i].reshape(1, C_HALF), c_ln_b[i].reshape(1, C_HALF)
        yc_p = _cmlp(proj_uv, ln_g, ln_b, c_w_s[i], c_b_s[i].T, row0=0, n_rows=tp, chunk=C_CHUNK, want_vn=False)
        eye_s = jnp.eye(ns, dtype=F32)
        w_s_blk = jax.vmap(lambda w: jnp.kron(eye_s, w))(c_w_s[i][:, :lq, :lq])
        yc_s, cv_s = _cmlp(proj_uv, ln_g, ln_b, w_s_blk, jnp.tile(c_b_s[i].T[:lq], (ns, 1)),
                           row0=tp, n_rows=ts, chunk=ts, want_vn=True)
        cv_s = cv_s.reshape(ns, lq, C_HALF)
        y_a = jnp.concatenate([ya_p, ya_s])
        y_b = jnp.concatenate([yb_p, yb_s.astype(BF16)])
        y_c = jnp.concatenate([yc_p, yc_s])
        merged = _gated_merge(y_a, y_b, y_c, w_br_a, w_br_b, w_br_c, i, proj_g, bm=BM, bn=BN)
        h, xf = _residual_norm(h, _matmul(merged, w_o, i, bm=BM, bn=BN), norm_mix_post[i], norm_ffn_pre[i])
        act, tail_g, tail_u, hs_g, hs_u = _ffn_up(
            xf, w_up, i, ffn_conv_w[i], ffn_conv_b[i].reshape(1, 2 * D_FF), _seam_rows(state_ffn_conv[i], lq),
            n_seq=nb, seq_len=ls, sample_rows=ts, sample_len=lq, bm=FFN_BM, bn=FFN_BN)
        tails = jnp.concatenate([tail_g, tail_u], axis=1).reshape(nb, ls // FFN_BM, 8, 2 * D_FF)
        fb_p = tails[:, -1, 8 - (FFN_CONV - 1):]
        fb_s = jnp.concatenate([hs_g, hs_u], axis=1).reshape(ns, lq, 2 * D_FF)[:, lq - (FFN_CONV - 1):]
        f = None
        for c in range(D_FF // D_MODEL):
            f = _matmul(act, w_down, i, bm=BM, bn=BN, k_chunk=c, bk=D_MODEL, acc=f)
        h, hn = _residual_norm(h, f, norm_ffn_post[i], norm_ple[i])
        p_i = jnp.concatenate([p_prompt[i].reshape(tp, ple), p_sample[i].reshape(ts, ple)]).astype(BF16)
        h = _ple_update(hn, p_i, w_ple_gate, w_ple, i, h, bm=BM, bn=BN)
        st_p.append((abuf_p, s_p, k_p, v_p, fb_p))
        st_s.append((abuf_s, s_s, k_s, v_s, fb_s, cv_s))

    def stack(states, j):
        return jnp.stack([s[j] for s in states])

    y_prompt = h[:tp].reshape(nb, ls, D_MODEL)
    y_sample = h[tp:].reshape(ns, lq, D_MODEL)
    return (y_prompt, y_sample,
            stack(st_p, 0), stack(st_p, 1), stack(st_p, 2), stack(st_p, 3), stack(st_p, 4),
            stack(st_s, 0), stack(st_s, 1), stack(st_s, 2), stack(st_s, 3), stack(st_s, 4),
            stack(st_s, 5))
```

```python
import functools
import math

import jax
import jax.numpy as jnp
from jax import lax
from jax.experimental import pallas as pl
from jax.experimental.pallas import tpu as pltpu

D_MODEL = 4096
DEPTH = 4
A_HEADS = 8
A_DK = 128
A_DV = 128
A_QK = A_HEADS * A_DK
A_VAL = A_HEADS * A_DV
A_CONV = 4
A_CONV_CH = 2 * A_QK + A_VAL
DELTA_CHUNK = 64
B_HEADS = 16
B_KV_HEADS = 4
B_GROUP = B_HEADS // B_KV_HEADS
B_HD = 64
B_Q = B_HEADS * B_HD
B_KV = B_KV_HEADS * B_HD
WINDOW = 128
C_GROUPS = 8
C_GC = 128
C_HALF = C_GROUPS * C_GC
C_CHUNK = 128
D_FF = 3 * D_MODEL
FFN_CONV = 3
N_BRANCH = 3
EPS = 1e-6

F32 = jnp.float32
BF16 = jnp.bfloat16
LANES = 128

VMEM_LIMIT_BYTES = 56 * 1024 * 1024

DELTA_RT = 256
DELTA_HP = 8
NEUMANN_BLOCK = 16
SAMPLE_LOCKSTEP = 4
NORM_ROWS = 264

BM = 1056
BN = 512
FFN_BM = 1024
FFN_BN = 256
OFF_BA = A_CONV_CH + A_VAL
SHIFT = 2 * A_HEADS
OFF_BQ = OFF_BA
OFF_KV = OFF_BQ + B_Q
OFF_CUV = OFF_KV + 2 * B_KV
OFF_GATES = OFF_CUV + 2 * C_HALF
BN_WIDE = 1024
CAST_ROWS = 512


def _params(n_axes):
    return pltpu.CompilerParams(dimension_semantics=("arbitrary",) * n_axes, vmem_limit_bytes=VMEM_LIMIT_BYTES)


def _sigmoid(x):
    return 1.0 / (1.0 + jnp.exp(-x))


def _softplus(x):
    return jnp.maximum(x, 0.0) + jnp.log(1.0 + jnp.exp(-jnp.abs(x)))


def _gelu_tanh(x):
    return 0.5 * x * (1.0 + jnp.tanh(0.7978845608028654 * (x + 0.044715 * (x * x * x))))


def _rms(x, g):
    return x * lax.rsqrt(jnp.mean(x * x, axis=-1, keepdims=True) + EPS) * g


def _dot(a, b):
    return jnp.dot(a, b, preferred_element_type=F32)


def _dot_nt(a, b):
    return lax.dot_general(a, b, (((1,), (1,)), ((), ())), preferred_element_type=F32)


def _dot_tn(a, b):
    return lax.dot_general(a, b, (((0,), (0,)), ((), ())), preferred_element_type=F32)


def _split2(x):
    hi = x.astype(BF16)
    lo = (x - hi.astype(F32)).astype(BF16)
    return hi, lo


def _dot3s(a, b):
    ah, al = a
    bh, bl = b
    return _dot(ah, bh) + (_dot(ah, bl) + _dot(al, bh))


def _mm_kernel(x_ref, w_ref, o_ref, wbf_ref):
    @pl.when(pl.program_id(1) == 0)
    def _():
        wbf_ref[...] = w_ref[...].astype(BF16)

    o_ref[...] = _dot(x_ref[...], wbf_ref[...])


def _mm_acc_kernel(x_ref, w_ref, a_ref, o_ref, wbf_ref):
    @pl.when(pl.program_id(1) == 0)
    def _():
        wbf_ref[...] = w_ref[...].astype(BF16)

    o_ref[...] = a_ref[...] + _dot(x_ref[...], wbf_ref[...])


def _mm_bf16_kernel(x_ref, w_ref, o_ref):
    o_ref[...] = _dot(x_ref[...], w_ref[...])


def _matmul(x, w, layer, *, bm, bn, n_cols=None, col0=0, k_chunk=0, bk=None, acc=None):
    m = x.shape[0]
    bk = w.shape[1] if bk is None else bk
    n_cols = w.shape[2] if n_cols is None else n_cols
    assert m % bm == 0 and n_cols % bn == 0 and col0 % bn == 0
    cb0 = col0 // bn
    grid = (n_cols // bn, m // bm)
    x_spec = pl.BlockSpec((bm, bk), lambda j, i: (i, k_chunk))
    w_spec = pl.BlockSpec((None, bk, bn), lambda j, i: (layer, k_chunk, cb0 + j))
    o_spec = pl.BlockSpec((bm, bn), lambda j, i: (i, j))
    out_shape = jax.ShapeDtypeStruct((m, n_cols), F32)
    if w.dtype == BF16:
        assert acc is None
        return pl.pallas_call(_mm_bf16_kernel, grid=grid, in_specs=[x_spec, w_spec], out_specs=o_spec,
                              out_shape=out_shape, compiler_params=_params(2), name="proj_bf16")(x, w)
    scratch = [pltpu.VMEM((bk, bn), BF16)]
    if acc is None:
        return pl.pallas_call(_mm_kernel, grid=grid, in_specs=[x_spec, w_spec], out_specs=o_spec,
                              out_shape=out_shape, scratch_shapes=scratch, compiler_params=_params(2),
                              name="proj")(x, w)
    return pl.pallas_call(_mm_acc_kernel, grid=grid, in_specs=[x_spec, w_spec, o_spec], out_specs=o_spec,
                          out_shape=out_shape, scratch_shapes=scratch, compiler_params=_params(2),
                          input_output_aliases={2: 0}, name="proj_acc")(x, w, acc)


def _mm_shift_kernel(x_ref, wa_ref, wb_ref, o_ref, wsh_ref, *, shift):
    @pl.when(pl.program_id(1) == 0)
    def _():
        k, bn = wa_ref.shape
        for r0 in range(0, k, CAST_ROWS):
            rs = slice(r0, r0 + CAST_ROWS)
            w = jnp.concatenate([wa_ref[rs, :], wb_ref[rs, :]], axis=1).astype(F32)
            wsh_ref[rs, :] = pltpu.roll(w, w.shape[1] - shift, axis=1)[:, :bn].astype(BF16)

    o_ref[...] = _dot(x_ref[...], wsh_ref[...])


def _matmul_shifted(x, w, layer, *, bm, bn, col0, n_cols, shift):
    m, k = x.shape
    assert m % bm == 0 and n_cols % bn == 0 and col0 % bn == 0 and 0 < shift < LANES and k % CAST_ROWS == 0
    assert w.dtype == BF16
    cb0 = col0 // bn
    lpb = bn // LANES
    return pl.pallas_call(
        functools.partial(_mm_shift_kernel, shift=shift), grid=(n_cols // bn, m // bm),
        in_specs=[pl.BlockSpec((bm, k), lambda j, i: (i, 0)),
                  pl.BlockSpec((None, k, bn), lambda j, i: (layer, 0, cb0 + j)),
                  pl.BlockSpec((None, k, LANES), lambda j, i: (layer, 0, (cb0 + j + 1) * lpb))],
        out_specs=pl.BlockSpec((bm, bn), lambda j, i: (i, j)),
        out_shape=jax.ShapeDtypeStruct((m, n_cols), F32),
        scratch_shapes=[pltpu.VMEM((k, bn), BF16)],
        compiler_params=_params(2), name="proj_shift",
    )(x, w, w)


def _norm_kernel(h_ref, g_ref, xn_ref):
    xn_ref[...] = _rms(h_ref[...], g_ref[...]).astype(xn_ref.dtype)


def _resnorm_kernel(h_ref, y_ref, gp_ref, gn_ref, ho_ref, xn_ref):
    h = h_ref[...] + _rms(y_ref[...], gp_ref[...])
    ho_ref[...] = h
    xn_ref[...] = _rms(h, gn_ref[...]).astype(xn_ref.dtype)


def _rms_norm_bf16(h, g):
    t, d = h.shape
    assert t % NORM_ROWS == 0
    row = pl.BlockSpec((NORM_ROWS, d), lambda r: (r, 0))
    vec = pl.BlockSpec((1, d), lambda r: (0, 0))
    return pl.pallas_call(_norm_kernel, grid=(t // NORM_ROWS,), in_specs=[row, vec], out_specs=row,
                          out_shape=jax.ShapeDtypeStruct((t, d), BF16), compiler_params=_params(1),
                          name="rms_norm")(h, g.reshape(1, d))


def _residual_norm(h, y, g_post, g_next):
    t, d = h.shape
    assert t % NORM_ROWS == 0
    row = pl.BlockSpec((NORM_ROWS, d), lambda r: (r, 0))
    vec = pl.BlockSpec((1, d), lambda r: (0, 0))
    return pl.pallas_call(
        _resnorm_kernel, grid=(t // NORM_ROWS,), in_specs=[row, row, vec, vec], out_specs=[row, row],
        out_shape=[jax.ShapeDtypeStruct((t, d), F32), jax.ShapeDtypeStruct((t, d), BF16)],
        input_output_aliases={0: 0}, compiler_params=_params(1), name="residual_norm",
    )(h, y, g_post.reshape(1, d), g_next.reshape(1, d))


def _run(gen):
    try:
        while True:
            next(gen)
    except StopIteration as stop:
        return stop.value


def _lockstep(gens):
    results = [None] * len(gens)
    live = list(range(len(gens)))
    while live:
        still = []
        for idx in live:
            try:
                next(gens[idx])
                still.append(idx)
            except StopIteration as stop:
                results[idx] = stop.value
        live = still
    return results


def _unit_lower_inverse(a, rx, c):
    eye = (rx == 0).astype(F32)
    blk = min(NEUMANN_BLOCK, c)
    n = -jnp.where(rx < blk, a, 0.0)
    p = eye + n
    pw = n
    for _ in range(int(math.log2(blk)) - 1):
        pws = _split2(pw)
        pw = _dot3s(pws, pws)
        yield
        p = p + _dot3s(_split2(p), _split2(pw))
        yield
    while blk < c:
        e = jnp.where((rx >= blk) & (rx < 2 * blk), a, 0.0)
        ps = _split2(p)
        ep = _dot3s(_split2(e), ps)
        yield
        p = p - _dot3s(ps, _split2(ep))
        yield
        blk *= 2
    return p


def _shift_rows(x, prev, j, pos):
    rp = pltpu.roll(prev, j, axis=0)
    if prev.shape[0] != x.shape[0]:
        rp = jnp.broadcast_to(rp[None], (x.shape[0] // 8, 8, x.shape[1])).reshape(x.shape)
    return jnp.where(pos < j, rp, pltpu.roll(x, j, axis=0))


def _conv_silu(x, xprev, cw, seg):
    pos = lax.broadcasted_iota(jnp.int32, x.shape, 0) & (seg - 1)
    acc = x * cw[3:4, :]
    for j in range(1, 4):
        acc = acc + _shift_rows(x, xprev, j, pos) * cw[3 - j:4 - j, :]
    return acc * _sigmoid(acc)


def _delta_tile_prep(q_raw, k_raw, v_raw, pq, pk, pv, cwq, cwk, cwv, beta_col, g_col, c, seg):
    rt = q_raw.shape[0]
    q = _conv_silu(q_raw, pq, cwq, seg)
    k = _conv_silu(k_raw, pk, cwk, seg)
    v = _conv_silu(v_raw, pv, cwv, seg)
    q = q * (lax.rsqrt(jnp.sum(q * q, axis=-1, keepdims=True) + EPS) * (A_DK ** -0.5))
    k = k * lax.rsqrt(jnp.sum(k * k, axis=-1, keepdims=True) + EPS)
    yield

    r_i = lax.broadcasted_iota(jnp.int32, (rt, rt), 0)
    c_i = lax.broadcasted_iota(jnp.int32, (rt, rt), 1)
    rx = r_i ^ c_i
    same = rx < c
    lower = same & (c_i <= r_i)
    strict = same & (c_i < r_i)

    g1 = g_col.astype(BF16)
    r1 = g_col - g1.astype(F32)
    g2 = r1.astype(BF16)
    g3 = (r1 - g2.astype(F32)).astype(BF16)
    lmat = lower.astype(BF16)
    umat = (same & (r_i <= c_i)).astype(BF16)
    ones = jnp.ones((rt, rt), BF16)
    gc = None
    gr = None
    for gp in (g1, g2, g3):
        a = _dot(lmat, jnp.broadcast_to(gp, (rt, LANES)))
        b = _dot(ones, jnp.broadcast_to(gp, (rt, rt)) * umat)
        gc = a if gc is None else gc + a
        gr = b if gr is None else gr + b
    gc_full = jnp.concatenate([gc] * (rt // LANES), axis=1)
    decay = jnp.where(lower, jnp.exp(jnp.where(lower, gc_full - gr, 0.0)), 0.0)
    yield

    kb = k.astype(BF16)
    kk = _dot_nt(kb, kb)
    a_mat = jnp.where(strict, beta_col * kk * decay, 0.0)
    yield
    t_inv = yield from _unit_lower_inverse(a_mat, rx, c)
    rhs = jnp.concatenate([beta_col * v, (beta_col * jnp.exp(gc)) * k], axis=1)
    sol = _dot3s(_split2(t_inv), _split2(rhs))
    yield
    u_base = sol[:, :LANES]
    w_mat = sol[:, LANES:]
    qk = jnp.where(lower, _dot_nt(q.astype(BF16), kb) * decay, 0.0)
    q_dec = q * jnp.exp(gc)
    yield
    return k, gc, u_base, w_mat, qk, q_dec


def _gated_out_norm(o, z, og):
    return _rms(o, og) * (z * _sigmoid(z))


def _head_cols(ba, alog_row, dtb_row, h, n_heads):
    lane = lax.broadcasted_iota(jnp.int32, ba.shape, 1)
    sig = _sigmoid(ba)
    g_all = -jnp.exp(alog_row) * _softplus(ba + dtb_row)
    beta_col = jnp.sum(jnp.where(lane == h, sig, 0.0), axis=1, keepdims=True)
    g_col = jnp.sum(jnp.where(lane == n_heads + h, g_all, 0.0), axis=1, keepdims=True)
    return beta_col, g_col


def _delta_prompt_kernel(q_ref, k_ref, v_ref, z_ref, ba_ref, cwq_ref, cwk_ref, cwv_ref, alog_ref, dtb_ref, og_ref,
                         y_ref, sfin_ref, xprev_ref, s_ref, *, n_heads, chunk):
    j = pl.program_id(2)
    hp = pl.program_id(1)
    rt = q_ref.shape[0]

    @pl.when(j == 0)
    def _():
        xprev_ref[...] = jnp.zeros_like(xprev_ref)
        s_ref[...] = jnp.zeros_like(s_ref)

    ba = ba_ref[...]

    def head(hh):
        cs = slice(hh * LANES, (hh + 1) * LANES)
        beta_col, g_col = _head_cols(ba, alog_ref[...], dtb_ref[...], hp * DELTA_HP + hh, n_heads)
        q_raw, k_raw, v_raw = q_ref[:, cs], k_ref[:, cs], v_ref[:, cs]
        k, gc, u_base, w_mat, qk, q_dec = yield from _delta_tile_prep(
            q_raw, k_raw, v_raw, xprev_ref[0, :, cs], xprev_ref[1, :, cs], xprev_ref[2, :, cs],
            cwq_ref[:, cs], cwk_ref[:, cs], cwv_ref[:, cs], beta_col, g_col, chunk, rt)
        xprev_ref[0, :, cs] = q_raw[rt - 8:]
        xprev_ref[1, :, cs] = k_raw[rt - 8:]
        xprev_ref[2, :, cs] = v_raw[rt - 8:]
        s = s_ref[hh]
        us, os_ = [], []
        for cc in range(rt // chunk):
            rs = slice(cc * chunk, (cc + 1) * chunk)
            sb = s.astype(BF16)
            u_c = u_base[rs] - _dot(w_mat[rs].astype(BF16), sb)
            os_.append(_dot(q_dec[rs].astype(BF16), sb))
            yield
            gl = gc[(cc + 1) * chunk - 1:(cc + 1) * chunk, :]
            k_dec = k[rs] * jnp.exp(gl - gc[rs])
            s = jnp.exp(gl) * s + _dot_tn(k_dec.astype(BF16), u_c.astype(BF16))
            us.append(u_c)
            yield
        s_ref[hh] = s
        u_all = jnp.concatenate(us, axis=0)
        o = jnp.concatenate(os_, axis=0) + _dot(qk.astype(BF16), u_all.astype(BF16))
        yield
        y_ref[:, cs] = _gated_out_norm(o, z_ref[:, cs], og_ref[...]).astype(y_ref.dtype)

    _lockstep([head(hh) for hh in range(DELTA_HP)])

    @pl.when(j == pl.num_programs(2) - 1)
    def _():
        sfin_ref[0] = s_ref[...]


def _delta_prompt(proj_a, proj_ba, conv_w, alog_row, dtb_row, out_g, *, n_seq, seq_len, n_heads, chunk):
    rt, hp = DELTA_RT, DELTA_HP
    assert seq_len % rt == 0 and rt % chunk == 0 and n_heads % hp == 0
    tiles = seq_len // rt
    nhp = n_heads // hp
    wblk = hp * LANES

    def col_spec(base):
        return pl.BlockSpec((rt, wblk), lambda n, h, j: (n * tiles + j, base + h))

    def cw_spec(base):
        return pl.BlockSpec((4, wblk), lambda n, h, j: (0, base + h))

    row128 = pl.BlockSpec((1, LANES), lambda n, h, j: (0, 0))
    kern = functools.partial(_delta_prompt_kernel, n_heads=n_heads, chunk=chunk)
    return pl.pallas_call(
        kern, grid=(n_seq, nhp, tiles),
        in_specs=[col_spec(0), col_spec(nhp), col_spec(2 * nhp), col_spec(3 * nhp),
                  pl.BlockSpec((rt, LANES), lambda n, h, j: (n * tiles + j, 0)),
                  cw_spec(0), cw_spec(nhp), cw_spec(2 * nhp), row128, row128, row128],
        out_specs=[pl.BlockSpec((rt, wblk), lambda n, h, j: (n * tiles + j, h)),
                   pl.BlockSpec((1, hp, LANES, LANES), lambda n, h, j: (n, h, 0, 0))],
        out_shape=[jax.ShapeDtypeStruct((n_seq * seq_len, n_heads * LANES), BF16),
                   jax.ShapeDtypeStruct((n_seq, n_heads, LANES, LANES), F32)],
        scratch_shapes=[pltpu.VMEM((3, 8, wblk), F32), pltpu.VMEM((hp, LANES, LANES), F32)],
        compiler_params=_params(3), name="delta_prompt",
    )(proj_a, proj_a, proj_a, proj_a, proj_ba, conv_w, conv_w, conv_w, alog_row, dtb_row, out_g)


def _attn_prompt_kernel(q_ref, kc_ref, kp_ref, vc_ref, vp_ref, sk_ref, o_ref, *, blocks_per_seq):
    b = pl.program_id(0) % blocks_per_seq
    w = q_ref.shape[0]
    r_i = lax.broadcasted_iota(jnp.int32, (w, 2 * w), 0)
    c_i = lax.broadcasted_iota(jnp.int32, (w, 2 * w), 1)
    mask = (c_i > r_i) & (c_i <= r_i + w) & ((b > 0) | (c_i >= w))
    scale = B_HD ** -0.5
    for kvh in range(B_KV_HEADS):
        ks = slice(kvh * B_HD, (kvh + 1) * B_HD)
        kk = jnp.concatenate([kp_ref[:, ks], kc_ref[:, ks]], axis=0).astype(BF16)
        vv = jnp.concatenate([vp_ref[:, ks], vc_ref[:, ks]], axis=0).astype(BF16)
        for g in range(B_GROUP):
            h = kvh * B_GROUP + g
            hs = slice(h * B_HD, (h + 1) * B_HD)
            s = _dot_nt(q_ref[:, hs].astype(BF16), kk) * scale
            s = jnp.where(mask, s, -jnp.inf)
            sink = sk_ref[h]
            m = jnp.maximum(jnp.max(s, axis=-1, keepdims=True), sink)
            e = jnp.exp(s - m)
            p = e / (jnp.sum(e, axis=-1, keepdims=True) + jnp.exp(sink - m))
            o_ref[:, hs] = _dot(p.astype(BF16), vv).astype(o_ref.dtype)


def _attn_prompt(q_arr, kv_arr, sinks, *, n_seq, seq_len):
    w = WINDOW
    bps = seq_len // w
    cur = lambda r: (r, 0)
    return pl.pallas_call(
        functools.partial(_attn_prompt_kernel, blocks_per_seq=bps), grid=(n_seq * bps,),
        in_specs=[pl.BlockSpec((w, B_Q), cur),
                  pl.BlockSpec((w, B_KV), cur), pl.BlockSpec((w, B_KV), lambda r: (jnp.maximum(r - 1, 0), 0)),
                  pl.BlockSpec((w, B_KV), lambda r: (r, 1)), pl.BlockSpec((w, B_KV), lambda r: (jnp.maximum(r - 1, 0), 1)),
                  pl.BlockSpec(memory_space=pltpu.SMEM)],
        out_specs=pl.BlockSpec((w, B_Q), cur),
        out_shape=jax.ShapeDtypeStruct((n_seq * seq_len, B_Q), BF16),
        compiler_params=_params(1), name="attn_prompt",
    )(q_arr, kv_arr, kv_arr, kv_arr, kv_arr, sinks)


def _cmlp_kernel(u_ref, v_ref, lg_ref, lb_ref, ws_ref, bs_ref, o_ref, *vn_ref):
    v = _gelu_tanh(v_ref[...])
    mu = jnp.mean(v, axis=-1, keepdims=True)
    d = v - mu
    var = jnp.mean(d * d, axis=-1, keepdims=True)
    vn = d * lax.rsqrt(var + EPS) * lg_ref[...] + lb_ref[...]
    if vn_ref:
        vn_ref[0][...] = vn
    c = v.shape[0]
    tril = lax.broadcasted_iota(jnp.int32, (c, c), 1) <= lax.broadcasted_iota(jnp.int32, (c, c), 0)
    for g in range(C_GROUPS):
        gs = slice(g * C_GC, (g + 1) * C_GC)
        wc = jnp.where(tril, ws_ref[g], 0.0).astype(BF16)
        mixed = _dot(wc, vn[:, gs].astype(BF16)) + bs_ref[:, g:g + 1]
        o_ref[:, gs] = (_gelu_tanh(u_ref[:, gs]) * mixed).astype(o_ref.dtype)


def _cmlp(uv_arr, ln_g, ln_b, w_s, b_s_t, *, row0, n_rows, chunk, want_vn):
    assert row0 % chunk == 0 and n_rows % chunk == 0
    rb0 = row0 // chunk
    blk = pl.BlockSpec((chunk, C_HALF), lambda r: (r, 0))
    vec = pl.BlockSpec((1, C_HALF), lambda r: (0, 0))
    out_specs = [blk, blk] if want_vn else blk
    y_shape = jax.ShapeDtypeStruct((n_rows, C_HALF), BF16)
    out_shape = [y_shape, jax.ShapeDtypeStruct((n_rows, C_HALF), F32)] if want_vn else y_shape
    return pl.pallas_call(
        _cmlp_kernel, grid=(n_rows // chunk,),
        in_specs=[pl.BlockSpec((chunk, C_HALF), lambda r: (rb0 + r, 0)),
                  pl.BlockSpec((chunk, C_HALF), lambda r: (rb0 + r, 1)), vec, vec,
                  pl.BlockSpec((C_GROUPS, chunk, chunk), lambda r: (0, 0, 0)),
                  pl.BlockSpec((chunk, C_GROUPS), lambda r: (0, 0))],
        out_specs=out_specs, out_shape=out_shape, compiler_params=_params(1), name="cmlp",
    )(uv_arr, uv_arr, ln_g, ln_b, w_s, b_s_t)


def _delta_sample_kernel(q_ref, k_ref, v_ref, z_ref, ba_ref, pq_ref, pk_ref, pv_ref, cwq_ref, cwk_ref, cwv_ref,
                         alog_ref, dtb_ref, og_ref, s0_ref, y_ref, snew_ref,
                         ub_ref, wm_ref, qd_ref, kd_ref, eg_ref, u_ref, oi_ref, *, n_heads, seq):
    h = pl.program_id(0)
    rt = q_ref.shape[0]
    beta_col, g_col = _head_cols(ba_ref[...], alog_ref[...], dtb_ref[...], h, n_heads)
    k, gc, u_base, w_mat, qk, q_dec = _run(_delta_tile_prep(
        q_ref[...], k_ref[...], v_ref[...], pq_ref[...], pk_ref[...], pv_ref[...],
        cwq_ref[...], cwk_ref[...], cwv_ref[...], beta_col, g_col, seq, seq))
    r_i = lax.broadcasted_iota(jnp.int32, (rt, rt), 0)
    c_i = lax.broadcasted_iota(jnp.int32, (rt, rt), 1)
    sel = (c_i == (r_i | (seq - 1))).astype(BF16)
    g1 = gc.astype(BF16)
    r1 = gc - g1.astype(F32)
    g2 = r1.astype(BF16)
    g3 = (r1 - g2.astype(F32)).astype(BF16)
    gl = _dot(sel, g1) + (_dot(sel, g2) + _dot(sel, g3))
    pad = jnp.zeros((8, LANES), F32)
    for ref, val in ((ub_ref, u_base), (wm_ref, w_mat), (qd_ref, q_dec), (kd_ref, k * jnp.exp(gl - gc)),
                     (eg_ref, jnp.exp(gl))):
        ref[0:rt, :] = val
        ref[rt:rt + 8, :] = pad

    def one_sequence(s):
        r0 = pl.multiple_of(s * seq, seq)
        win = pl.ds(r0, 2 * seq)
        s0 = s0_ref[s]
        sb = s0.astype(BF16)
        first = lax.broadcasted_iota(jnp.int32, (2 * seq, LANES), 0) < seq
        u16 = ub_ref[win, :] - _dot(wm_ref[win, :].astype(BF16), sb)
        o16 = _dot(qd_ref[win, :].astype(BF16), sb)
        yield
        kd16 = jnp.where(first, kd_ref[win, :], 0.0)
        snew_ref[s] = eg_ref[pl.ds(r0, 1), :] * s0 + _dot_tn(kd16.astype(BF16), jnp.where(first, u16, 0.0).astype(BF16))
        u_ref[pl.ds(r0, seq), :] = u16[0:seq]
        oi_ref[pl.ds(r0, seq), :] = o16[0:seq]

    def body(i, carry):
        _lockstep([one_sequence(SAMPLE_LOCKSTEP * i + d) for d in range(SAMPLE_LOCKSTEP)])
        return carry

    lax.fori_loop(0, rt // seq // SAMPLE_LOCKSTEP, body, 0)
    o = oi_ref[...] + _dot(qk.astype(BF16), u_ref[...].astype(BF16))
    y_ref[...] = _gated_out_norm(o, z_ref[...], og_ref[...]).astype(y_ref.dtype)


def _delta_sample(proj_a, proj_ba, p_conv, conv_w, alog_row, dtb_row, out_g, s0, *, row0, n_seq, seq, n_heads):
    rt = n_seq * seq
    assert seq == 8 and row0 % rt == 0 and rt % LANES == 0
    rb = row0 // rt

    def col_spec(base):
        return pl.BlockSpec((rt, LANES), lambda h: (rb, base + h))

    def p_spec(base):
        return pl.BlockSpec((rt, LANES), lambda h: (0, base + h))

    def cw_spec(base):
        return pl.BlockSpec((4, LANES), lambda h: (0, base + h))

    row128 = pl.BlockSpec((1, LANES), lambda h: (0, 0))
    st_spec = pl.BlockSpec((n_seq, None, LANES, LANES), lambda h: (0, h, 0, 0))
    big = pltpu.VMEM((rt + 8, LANES), F32)
    kern = functools.partial(_delta_sample_kernel, n_heads=n_heads, seq=seq)
    return pl.pallas_call(
        kern, grid=(n_heads,),
        in_specs=[col_spec(0), col_spec(n_heads), col_spec(2 * n_heads), col_spec(3 * n_heads),
                  pl.BlockSpec((rt, LANES), lambda h: (rb, 0)),
                  p_spec(0), p_spec(n_heads), p_spec(2 * n_heads),
                  cw_spec(0), cw_spec(n_heads), cw_spec(2 * n_heads), row128, row128, row128, st_spec],
        out_specs=[pl.BlockSpec((rt, LANES), lambda h: (0, h)), st_spec],
        out_shape=[jax.ShapeDtypeStruct((rt, n_heads * LANES), BF16),
                   jax.ShapeDtypeStruct((n_seq, n_heads, LANES, LANES), F32)],
        scratch_shapes=[big, big, big, big, big, pltpu.VMEM((rt, LANES), F32), pltpu.VMEM((rt, LANES), F32)],
        compiler_params=_params(1), name="delta_sample",
    )(proj_a, proj_a, proj_a, proj_a, proj_ba, p_conv, p_conv, p_conv, conv_w, conv_w, conv_w,
      alog_row, dtb_row, out_g, s0)


def _attn_sample_kernel(q_ref, kvn_ref, kb_ref, vb_ref, sk_ref, o_ref):
    lq = q_ref.shape[0]
    wb = kb_ref.shape[0]
    nk = 2 * wb
    rows = B_GROUP * lq
    r_i = lax.broadcasted_iota(jnp.int32, (rows, nk), 0)
    c_i = lax.broadcasted_iota(jnp.int32, (rows, nk), 1)
    i = r_i & (lq - 1)
    mask = (c_i <= i + wb) & (c_i > i + wb - WINDOW)
    grp = lax.shift_right_logical(lax.broadcasted_iota(jnp.int32, (rows, 1), 0), int(math.log2(lq)))
    scale = B_HD ** -0.5
    zpad = jnp.zeros((nk - wb - lq, B_HD), F32)
    def kv_head(kvh):
        ks = slice(kvh * B_HD, (kvh + 1) * B_HD)
        vs = slice(B_KV + kvh * B_HD, B_KV + (kvh + 1) * B_HD)
        kk = jnp.concatenate([kb_ref[:, ks], kvn_ref[:, ks], zpad], axis=0).astype(BF16)
        vv = jnp.concatenate([vb_ref[:, ks], kvn_ref[:, vs], zpad], axis=0).astype(BF16)
        h0 = kvh * B_GROUP
        q = jnp.concatenate([q_ref[:, (h0 + g) * B_HD:(h0 + g + 1) * B_HD] for g in range(B_GROUP)], axis=0)
        s = jnp.where(mask, _dot_nt(q.astype(BF16), kk) * scale, -jnp.inf)
        yield
        sink = jnp.zeros((rows, 1), F32)
        for g in range(B_GROUP):
            sink = jnp.where(grp == g, sk_ref[h0 + g], sink)
        m = jnp.maximum(jnp.max(s, axis=-1, keepdims=True), sink)
        e = jnp.exp(s - m)
        p = e / (jnp.sum(e, axis=-1, keepdims=True) + jnp.exp(sink - m))
        yield
        o = _dot(p.astype(BF16), vv)
        yield
        for g in range(B_GROUP):
            o_ref[:, (h0 + g) * B_HD:(h0 + g + 1) * B_HD] = o[g * lq:(g + 1) * lq]

    _lockstep([kv_head(kvh) for kvh in range(B_KV_HEADS)])


def _attn_sample(q_arr, kv_arr, k_buf, v_buf, sinks, *, row0, n_seq, seq):
    assert row0 % seq == 0 and seq & (seq - 1) == 0 and k_buf.shape[1] >= seq
    rb0 = row0 // seq
    wb = k_buf.shape[1]
    buf = pl.BlockSpec((None, wb, B_KV), lambda s: (s, 0, 0))
    return pl.pallas_call(
        _attn_sample_kernel, grid=(n_seq,),
        in_specs=[pl.BlockSpec((seq, B_Q), lambda s: (rb0 + s, 0)), pl.BlockSpec((seq, 2 * B_KV), lambda s: (rb0 + s, 0)),
                  buf, buf, pl.BlockSpec(memory_space=pltpu.SMEM)],
        out_specs=pl.BlockSpec((seq, B_Q), lambda s: (s, 0)),
        out_shape=jax.ShapeDtypeStruct((n_seq * seq, B_Q), F32),
        compiler_params=_params(1), name="attn_sample",
    )(q_arr, kv_arr, k_buf, v_buf, sinks)


def _merge_kernel(ya_ref, yb_ref, yc_ref, wa_ref, wb_ref, wc_ref, ga_ref, gb_ref, gc_ref, o_ref,
                  wab_ref, wbb_ref, wcb_ref):
    @pl.when(pl.program_id(1) == 0)
    def _():
        wab_ref[...] = wa_ref[...].astype(BF16)
        wbb_ref[...] = wb_ref[...].astype(BF16)
        wcb_ref[...] = wc_ref[...].astype(BF16)

    acc = _sigmoid(ga_ref[...]) * _dot(ya_ref[...], wab_ref[...])
    acc = acc + _sigmoid(gb_ref[...]) * _dot(yb_ref[...], wbb_ref[...])
    acc = acc + _sigmoid(gc_ref[...]) * _dot(yc_ref[...], wcb_ref[...])
    o_ref[...] = acc.astype(o_ref.dtype)


def _gated_merge(y_a, y_b, y_c, w_a, w_b, w_c, layer, gate_logits, *, bm, bn):
    t, kdim = y_a.shape
    d = w_a.shape[2]
    assert t % bm == 0 and d % bn == 0
    nj = d // bn
    y_spec = pl.BlockSpec((bm, kdim), lambda j, i: (i, 0))
    w_spec = pl.BlockSpec((None, kdim, bn), lambda j, i: (layer, 0, j))

    def g_spec(b):
        return pl.BlockSpec((bm, bn), lambda j, i: (i, b * nj + j))

    return pl.pallas_call(
        _merge_kernel, grid=(nj, t // bm),
        in_specs=[y_spec, y_spec, y_spec, w_spec, w_spec, w_spec, g_spec(0), g_spec(1), g_spec(2)],
        out_specs=pl.BlockSpec((bm, bn), lambda j, i: (i, j)),
        out_shape=jax.ShapeDtypeStruct((t, d), BF16),
        scratch_shapes=[pltpu.VMEM((kdim, bn), BF16)] * 3,
        compiler_params=_params(2), name="gated_merge",
    )(y_a, y_b, y_c, w_a, w_b, w_c, gate_logits, gate_logits, gate_logits)


def _conv3(h, hprev, cw, b, pos):
    acc = h * cw[2:3, :] + b
    for j in (1, 2):
        acc = acc + _shift_rows(h, hprev, j, pos) * cw[2 - j:3 - j, :]
    return acc


def _ffn_up_kernel(x_ref, wg_ref, wu_ref, cwg_ref, cwu_ref, bg_ref, bu_ref, pg_ref, pu_ref,
                   act_ref, tg_ref, tu_ref, hsg_ref, hsu_ref, wgb_ref, wub_ref, prevg_ref, prevu_ref,
                   *, tiles_per_seq, n_prompt_tiles, sample_rows, sample_len):
    m = pl.program_id(1)
    bm = x_ref.shape[0]

    @pl.when(m == 0)
    def _():
        wgb_ref[...] = wg_ref[...].astype(BF16)
        wub_ref[...] = wu_ref[...].astype(BF16)

    @pl.when(m < n_prompt_tiles)
    def _():
        x = x_ref[...]
        hg = _dot(x, wgb_ref[...])
        hu = _dot(x, wub_ref[...])
        first = (m % tiles_per_seq) == 0
        pos = lax.broadcasted_iota(jnp.int32, hg.shape, 0)
        pg = jnp.where(first, 0.0, prevg_ref[...])
        pu = jnp.where(first, 0.0, prevu_ref[...])
        g = _conv3(hg, pg, cwg_ref[...], bg_ref[...], pos)
        u = _conv3(hu, pu, cwu_ref[...], bu_ref[...], pos)
        act_ref[...] = (_gelu_tanh(g) * u).astype(act_ref.dtype)
        prevg_ref[...] = hg
        prevu_ref[...] = hu
        tg_ref[...] = hg[bm - 8:, :]
        tu_ref[...] = hu[bm - 8:, :]

    @pl.when(m == n_prompt_tiles)
    def _():
        x = x_ref[0:sample_rows, :]
        hg = _dot(x, wgb_ref[...])
        hu = _dot(x, wub_ref[...])
        pos = lax.broadcasted_iota(jnp.int32, hg.shape, 0) & (sample_len - 1)
        g = _conv3(hg, pg_ref[...], cwg_ref[...], bg_ref[...], pos)
        u = _conv3(hu, pu_ref[...], cwu_ref[...], bu_ref[...], pos)
        act_ref[0:sample_rows, :] = (_gelu_tanh(g) * u).astype(act_ref.dtype)
        hsg_ref[...] = hg
        hsu_ref[...] = hu


def _ffn_up(x, w_up, layer, conv_w, conv_b, p_state, *, n_seq, seq_len, sample_rows, sample_len, bm, bn):
    t, d = x.shape
    f = w_up.shape[2] // 2
    tp = n_seq * seq_len
    assert seq_len % bm == 0 and f % bn == 0 and t == tp + sample_rows and sample_rows <= bm
    npt = tp // bm
    nj = f // bn
    kern = functools.partial(_ffn_up_kernel, tiles_per_seq=seq_len // bm, n_prompt_tiles=npt,
                             sample_rows=sample_rows, sample_len=sample_len)

    def half(rows, base):
        return pl.BlockSpec((rows, bn), lambda j, m: (0, base + j))

    def w_half(base):
        return pl.BlockSpec((None, d, bn), lambda j, m: (layer, 0, base + j))

    tail_spec = pl.BlockSpec((8, bn), lambda j, m: (jnp.minimum(m, npt - 1), j))
    return pl.pallas_call(
        kern, grid=(nj, npt + 1),
        in_specs=[pl.BlockSpec((bm, d), lambda j, m: (m, 0)),
                  w_half(0), w_half(nj), half(3, 0), half(3, nj), half(1, 0), half(1, nj),
                  half(sample_rows, 0), half(sample_rows, nj)],
        out_specs=[pl.BlockSpec((bm, bn), lambda j, m: (m, j)), tail_spec, tail_spec,
                   half(sample_rows, 0), half(sample_rows, 0)],
        out_shape=[jax.ShapeDtypeStruct((t, f), BF16),
                   jax.ShapeDtypeStruct((npt * 8, f), F32), jax.ShapeDtypeStruct((npt * 8, f), F32),
                   jax.ShapeDtypeStruct((sample_rows, f), F32), jax.ShapeDtypeStruct((sample_rows, f), F32)],
        scratch_shapes=[pltpu.VMEM((d, bn), BF16), pltpu.VMEM((d, bn), BF16),
                        pltpu.VMEM((bm, bn), F32), pltpu.VMEM((bm, bn), F32)],
        compiler_params=_params(2), name="ffn_up",
    )(x, w_up, w_up, conv_w, conv_w, conv_b, conv_b, p_state, p_state)


def _ple_kernel(x_ref, p_ref, wg_ref, wp_ref, h_ref, o_ref, wgb_ref, wpb_ref):
    @pl.when(pl.program_id(1) == 0)
    def _():
        wgb_ref[...] = wg_ref[...].astype(BF16)
        wpb_ref[...] = wp_ref[...].astype(BF16)

    gate = _sigmoid(_dot(x_ref[...], wgb_ref[...]))
    o_ref[...] = h_ref[...] + gate * _dot(p_ref[...], wpb_ref[...])


def _ple_update(hn, p, w_gate, w_ple, layer, h, *, bm, bn):
    t, d = h.shape
    kp = p.shape[1]
    assert t % bm == 0 and d % bn == 0
    blk = pl.BlockSpec((bm, bn), lambda j, i: (i, j))
    return pl.pallas_call(
        _ple_kernel, grid=(d // bn, t // bm),
        in_specs=[pl.BlockSpec((bm, d), lambda j, i: (i, 0)), pl.BlockSpec((bm, kp), lambda j, i: (i, 0)),
                  pl.BlockSpec((None, d, bn), lambda j, i: (layer, 0, j)),
                  pl.BlockSpec((None, kp, bn), lambda j, i: (layer, 0, j)), blk],
        out_specs=blk, out_shape=jax.ShapeDtypeStruct((t, d), F32),
        scratch_shapes=[pltpu.VMEM((d, bn), BF16), pltpu.VMEM((kp, bn), BF16)],
        input_output_aliases={4: 0}, compiler_params=_params(2), name="ple_update",
    )(hn, p, w_gate, w_ple, h)


def _seam_rows(state, seq):
    n_seq, w, c = state.shape
    padded = jnp.pad(state, ((0, 0), (seq - w, 0), (0, 0)))
    return jnp.roll(padded, -1, axis=0).reshape(n_seq * seq, c)
def kernel(x_prompt, x_sample, p_prompt, p_sample, state_a_conv, state_delta, cache_win_k, cache_win_v, state_ffn_conv, norm_mix_pre, norm_mix_post, norm_ffn_pre, norm_ffn_post, norm_ple, w_in, a_conv_w, a_log, a_dt_bias, a_out_norm, b_sinks, c_ln_g, c_ln_b, c_w_s, c_b_s, w_br_a, w_br_b, w_br_c, w_o, w_up, ffn_conv_w, ffn_conv_b, w_down, w_ple, w_ple_gate):
    nb, ls = x_prompt.shape[:2]
    ns, lq = x_sample.shape[:2]
    tp, ts = nb * ls, ns * lq
    ple = p_prompt.shape[-1]
    h = jnp.concatenate([x_prompt.reshape(tp, D_MODEL), x_sample.reshape(ts, D_MODEL)], axis=0)
    w_in_bf = w_in.astype(BF16)
    shifted = functools.partial(_matmul_shifted, bm=BM, bn=BN, shift=SHIFT)
    st_p, st_s = [], []
    for i in range(DEPTH):
        xn = _rms_norm_bf16(h, norm_mix_pre[i])
        proj_a = _matmul(xn, w_in_bf, i, bm=BM, bn=BN_WIDE, n_cols=OFF_BA)
        proj_ba = _matmul(xn, w_in_bf, i, bm=BM, bn=LANES, col0=OFF_BA, n_cols=LANES)
        proj_q = shifted(xn, w_in_bf, i, col0=OFF_BQ, n_cols=B_Q)
        proj_kv = shifted(xn, w_in_bf, i, col0=OFF_KV, n_cols=2 * B_KV)
        proj_uv = shifted(xn, w_in_bf, i, col0=OFF_CUV, n_cols=2 * C_HALF)
        proj_g = shifted(xn, w_in_bf, i, col0=OFF_GATES, n_cols=N_BRANCH * D_MODEL)
        alog_row = jnp.zeros((1, LANES), F32).at[0, A_HEADS:2 * A_HEADS].set(a_log[i])
        dtb_row = jnp.zeros((1, LANES), F32).at[0, A_HEADS:2 * A_HEADS].set(a_dt_bias[i])
        ya_p, s_p = _delta_prompt(proj_a, proj_ba, a_conv_w[i], alog_row, dtb_row, a_out_norm[i].reshape(1, A_DV),
                                  n_seq=nb, seq_len=ls, n_heads=A_HEADS, chunk=math.gcd(ls, DELTA_CHUNK))
        abuf_p = jnp.stack([proj_a[(n + 1) * ls - (A_CONV - 1):(n + 1) * ls, :A_CONV_CH] for n in range(nb)])
        ya_s, s_s = _delta_sample(proj_a, proj_ba, _seam_rows(state_a_conv[i], lq), a_conv_w[i], alog_row, dtb_row,
                                  a_out_norm[i].reshape(1, A_DV), state_delta[i],
                                  row0=tp, n_seq=ns, seq=lq, n_heads=A_HEADS)
        abuf_s = proj_a[tp:, :A_CONV_CH].reshape(ns, lq, A_CONV_CH)[:, lq - (A_CONV - 1):]
        yb_p = _attn_prompt(proj_q, proj_kv, b_sinks[i], n_seq=nb, seq_len=ls)
        wb = min(WINDOW, ls)
        kv_tail = proj_kv[:tp].reshape(nb, ls, 2, B_KV_HEADS, B_HD)[:, ls - wb:]
        k_p, v_p = kv_tail[:, :, 0], kv_tail[:, :, 1]
        wbs = cache_win_k.shape[2]
        yb_s = _attn_sample(proj_q, proj_kv, cache_win_k[i].reshape(ns, wbs, B_KV), cache_win_v[i].reshape(ns, wbs, B_KV),
                            b_sinks[i], row0=tp, n_seq=ns, seq=lq)
        kv_s = proj_kv[tp:].reshape(ns, lq, 2, B_KV_HEADS, B_HD)
        k_s = jnp.concatenate([cache_win_k[i], kv_s[:, :, 0]], axis=1)[:, lq:]
        v_s = jnp.concatenate([cache_win_v[i], kv_s[:, :, 1]], axis=1)[:, lq:]
        ln_g, ln_b = c_ln_g[i].reshape(1, C_HALF), c_ln_b[i].reshape(1, C_HALF)
        yc_p = _cmlp(proj_uv, ln_g, ln_b, c_w_s[i], c_b_s[i].T, row0=0, n_rows=tp, chunk=C_CHUNK, want_vn=False)
        eye_s = jnp.eye(ns, dtype=F32)
        w_s_blk = jax.vmap(lambda w: jnp.kron(eye_s, w))(c_w_s[i][:, :lq, :lq])
        yc_s, cv_s = _cmlp(proj_uv, ln_g, ln_b, w_s_blk, jnp.tile(c_b_s[i].T[:lq], (ns, 1)),
                           row0=tp, n_rows=ts, chunk=ts, want_vn=True)
        cv_s = cv_s.reshape(ns, lq, C_HALF)
        y_a = jnp.concatenate([ya_p, ya_s])
        y_b = jnp.concatenate([yb_p, yb_s.astype(BF16)])
        y_c = jnp.concatenate([yc_p, yc_s])
        merged = _gated_merge(y_a, y_b, y_c, w_br_a, w_br_b, w_br_c, i, proj_g, bm=BM, bn=BN)
        h, xf = _residual_norm(h, _matmul(merged, w_o, i, bm=BM, bn=BN), norm_mix_post[i], norm_ffn_pre[i])
        act, tail_g, tail_u, hs_g, hs_u = _ffn_up(
            xf, w_up, i, ffn_conv_w[i], ffn_conv_b[i].reshape(1, 2 * D_FF), _seam_rows(state_ffn_conv[i], lq),
            n_seq=nb, seq_len=ls, sample_rows=ts, sample_len=lq, bm=FFN_BM, bn=FFN_BN)
        tails = jnp.concatenate([tail_g, tail_u], axis=1).reshape(nb, ls // FFN_BM, 8, 2 * D_FF)
        fb_p = tails[:, -1, 8 - (FFN_CONV - 1):]
        fb_s = jnp.concatenate([hs_g, hs_u], axis=1).reshape(ns, lq, 2 * D_FF)[:, lq - (FFN_CONV - 1):]
        f = None
        for c in range(D_FF // D_MODEL):
            f = _matmul(act, w_down, i, bm=BM, bn=BN, k_chunk=c, bk=D_MODEL, acc=f)
        h, hn = _residual_norm(h, f, norm_ffn_post[i], norm_ple[i])
        p_i = jnp.concatenate([p_prompt[i].reshape(tp, ple), p_sample[i].reshape(ts, ple)]).astype(BF16)
        h = _ple_update(hn, p_i, w_ple_gate, w_ple, i, h, bm=BM, bn=BN)
        st_p.append((abuf_p, s_p, k_p, v_p, fb_p))
        st_s.append((abuf_s, s_s, k_s, v_s, fb_s, cv_s))

    def stack(states, j):
        return jnp.stack([s[j] for s in states])

    y_prompt = h[:tp].reshape(nb, ls, D_MODEL)
    y_sample = h[tp:].reshape(ns, lq, D_MODEL)
    return (y_prompt, y_sample,
            stack(st_p, 0), stack(st_p, 1), stack(st_p, 2), stack(st_p, 3), stack(st_p, 4),
            stack(st_s, 0), stack(st_s, 1), stack(st_s, 2), stack(st_s, 3), stack(st_s, 4),
            stack(st_s, 5))
```

```python
import functools
import math

import jax
import jax.numpy as jnp
from jax import lax
from jax.experimental import pallas as pl
from jax.experimental.pallas import tpu as pltpu

D_MODEL = 4096
DEPTH = 4
A_HEADS = 8
A_DK = 128
A_DV = 128
A_QK = A_HEADS * A_DK
A_VAL = A_HEADS * A_DV
A_CONV = 4
A_CONV_CH = 2 * A_QK + A_VAL
DELTA_CHUNK = 64
B_HEADS = 16
B_KV_HEADS = 4
B_GROUP = B_HEADS // B_KV_HEADS
B_HD = 64
B_Q = B_HEADS * B_HD
B_KV = B_KV_HEADS * B_HD
WINDOW = 128
C_GROUPS = 8
C_GC = 128
C_HALF = C_GROUPS * C_GC
C_CHUNK = 128
D_FF = 3 * D_MODEL
FFN_CONV = 3
N_BRANCH = 3
EPS = 1e-6

F32 = jnp.float32
BF16 = jnp.bfloat16
LANES = 128

VMEM_LIMIT_BYTES = 56 * 1024 * 1024

DELTA_RT = 256
DELTA_HP = 8
NEUMANN_BLOCK = 16
SAMPLE_LOCKSTEP = 4
NORM_ROWS = 264

BM = 1056
BN = 512
FFN_BM = 1024
FFN_BN = 256
OFF_BA = A_CONV_CH + A_VAL
SHIFT = 2 * A_HEADS
OFF_BQ = OFF_BA
OFF_KV = OFF_BQ + B_Q
OFF_CUV = OFF_KV + 2 * B_KV
OFF_GATES = OFF_CUV + 2 * C_HALF
BN_WIDE = 1024
CAST_ROWS = 512


def _params(n_axes):
    return pltpu.CompilerParams(dimension_semantics=("arbitrary",) * n_axes, vmem_limit_bytes=VMEM_LIMIT_BYTES)


def _sigmoid(x):
    return 1.0 / (1.0 + jnp.exp(-x))


def _softplus(x):
    return jnp.maximum(x, 0.0) + jnp.log(1.0 + jnp.exp(-jnp.abs(x)))


def _gelu_tanh(x):
    return 0.5 * x * (1.0 + jnp.tanh(0.7978845608028654 * (x + 0.044715 * (x * x * x))))


def _rms(x, g):
    return x * lax.rsqrt(jnp.mean(x * x, axis=-1, keepdims=True) + EPS) * g


def _dot(a, b):
    return jnp.dot(a, b, preferred_element_type=F32)


def _dot_nt(a, b):
    return lax.dot_general(a, b, (((1,), (1,)), ((), ())), preferred_element_type=F32)


def _dot_tn(a, b):
    return lax.dot_general(a, b, (((0,), (0,)), ((), ())), preferred_element_type=F32)


def _split2(x):
    hi = x.astype(BF16)
    lo = (x - hi.astype(F32)).astype(BF16)
    return hi, lo


def _dot3s(a, b):
    ah, al = a
    bh, bl = b
    return _dot(ah, bh) + (_dot(ah, bl) + _dot(al, bh))


def _mm_kernel(x_ref, w_ref, o_ref, wbf_ref):
    @pl.when(pl.program_id(1) == 0)
    def _():
        wbf_ref[...] = w_ref[...].astype(BF16)

    o_ref[...] = _dot(x_ref[...], wbf_ref[...])


def _mm_acc_kernel(x_ref, w_ref, a_ref, o_ref, wbf_ref):
    @pl.when(pl.program_id(1) == 0)
    def _():
        wbf_ref[...] = w_ref[...].astype(BF16)

    o_ref[...] = a_ref[...] + _dot(x_ref[...], wbf_ref[...])


def _mm_bf16_kernel(x_ref, w_ref, o_ref):
    o_ref[...] = _dot(x_ref[...], w_ref[...])


def _matmul(x, w, layer, *, bm, bn, n_cols=None, col0=0, k_chunk=0, bk=None, acc=None):
    m = x.shape[0]
    bk = w.shape[1] if bk is None else bk
    n_cols = w.shape[2] if n_cols is None else n_cols
    assert m % bm == 0 and n_cols % bn == 0 and col0 % bn == 0
    cb0 = col0 // bn
    grid = (n_cols // bn, m // bm)
    x_spec = pl.BlockSpec((bm, bk), lambda j, i: (i, k_chunk))
    w_spec = pl.BlockSpec((None, bk, bn), lambda j, i: (layer, k_chunk, cb0 + j))
    o_spec = pl.BlockSpec((bm, bn), lambda j, i: (i, j))
    out_shape = jax.ShapeDtypeStruct((m, n_cols), F32)
    if w.dtype == BF16:
        assert acc is None
        return pl.pallas_call(_mm_bf16_kernel, grid=grid, in_specs=[x_spec, w_spec], out_specs=o_spec,
                              out_shape=out_shape, compiler_params=_params(2), name="proj_bf16")(x, w)
    scratch = [pltpu.VMEM((bk, bn), BF16)]
    if acc is None:
        return pl.pallas_call(_mm_kernel, grid=grid, in_specs=[x_spec, w_spec], out_specs=o_spec,
                              out_shape=out_shape, scratch_shapes=scratch, compiler_params=_params(2),
                              name="proj")(x, w)
    return pl.pallas_call(_mm_acc_kernel, grid=grid, in_specs=[x_spec, w_spec, o_spec], out_specs=o_spec,
                          out_shape=out_shape, scratch_shapes=scratch, compiler_params=_params(2),
                          input_output_aliases={2: 0}, name="proj_acc")(x, w, acc)


def _mm_shift_kernel(x_ref, wa_ref, wb_ref, o_ref, wsh_ref, *, shift):
    @pl.when(pl.program_id(1) == 0)
    def _():
        k, bn = wa_ref.shape
        for r0 in range(0, k, CAST_ROWS):
            rs = slice(r0, r0 + CAST_ROWS)
            w = jnp.concatenate([wa_ref[rs, :], wb_ref[rs, :]], axis=1).astype(F32)
            wsh_ref[rs, :] = pltpu.roll(w, w.shape[1] - shift, axis=1)[:, :bn].astype(BF16)

    o_ref[...] = _dot(x_ref[...], wsh_ref[...])


def _matmul_shifted(x, w, layer, *, bm, bn, col0, n_cols, shift):
    m, k = x.shape
    assert m % bm == 0 and n_cols % bn == 0 and col0 % bn == 0 and 0 < shift < LANES and k % CAST_ROWS == 0
    assert w.dtype == BF16
    cb0 = col0 // bn
    lpb = bn // LANES
    return pl.pallas_call(
        functools.partial(_mm_shift_kernel, shift=shift), grid=(n_cols // bn, m // bm),
        in_specs=[pl.BlockSpec((bm, k), lambda j, i: (i, 0)),
                  pl.BlockSpec((None, k, bn), lambda j, i: (layer, 0, cb0 + j)),
                  pl.BlockSpec((None, k, LANES), lambda j, i: (layer, 0, (cb0 + j + 1) * lpb))],
        out_specs=pl.BlockSpec((bm, bn), lambda j, i: (i, j)),
        out_shape=jax.ShapeDtypeStruct((m, n_cols), F32),
        scratch_shapes=[pltpu.VMEM((k, bn), BF16)],
        compiler_params=_params(2), name="proj_shift",
    )(x, w, w)


def _norm_kernel(h_ref, g_ref, xn_ref):
    xn_ref[...] = _rms(h_ref[...], g_ref[...]).astype(xn_ref.dtype)


def _resnorm_kernel(h_ref, y_ref, gp_ref, gn_ref, ho_ref, xn_ref):
    h = h_ref[...] + _rms(y_ref[...], gp_ref[...])
    ho_ref[...] = h
    xn_ref[...] = _rms(h, gn_ref[...]).astype(xn_ref.dtype)


def _rms_norm_bf16(h, g):
    t, d = h.shape
    assert t % NORM_ROWS == 0
    row = pl.BlockSpec((NORM_ROWS, d), lambda r: (r, 0))
    vec = pl.BlockSpec((1, d), lambda r: (0, 0))
    return pl.pallas_call(_norm_kernel, grid=(t // NORM_ROWS,), in_specs=[row, vec], out_specs=row,
                          out_shape=jax.ShapeDtypeStruct((t, d), BF16), compiler_params=_params(1),
                          name="rms_norm")(h, g.reshape(1, d))


def _residual_norm(h, y, g_post, g_next):
    t, d = h.shape
    assert t % NORM_ROWS == 0
    row = pl.BlockSpec((NORM_ROWS, d), lambda r: (r, 0))
    vec = pl.BlockSpec((1, d), lambda r: (0, 0))
    return pl.pallas_call(
        _resnorm_kernel, grid=(t // NORM_ROWS,), in_specs=[row, row, vec, vec], out_specs=[row, row],
        out_shape=[jax.ShapeDtypeStruct((t, d), F32), jax.ShapeDtypeStruct((t, d), BF16)],
        input_output_aliases={0: 0}, compiler_params=_params(1), name="residual_norm",
    )(h, y, g_post.reshape(1, d), g_next.reshape(1, d))


def _run(gen):
    try:
        while True:
            next(gen)
    except StopIteration as stop:
        return stop.value


def _lockstep(gens):
    results = [None] * len(gens)
    live = list(range(len(gens)))
    while live:
        still = []
        for idx in live:
            try:
                next(gens[idx])
                still.append(idx)
            except StopIteration as stop:
                results[idx] = stop.value
        live = still
    return results


def _unit_lower_inverse(a, rx, c):
    eye = (rx == 0).astype(F32)
    blk = min(NEUMANN_BLOCK, c)
    n = -jnp.where(rx < blk, a, 0.0)
    p = eye + n
    pw = n
    for _ in range(int(math.log2(blk)) - 1):
        pws = _split2(pw)
        pw = _dot3s(pws, pws)
        yield
        p = p + _dot3s(_split2(p), _split2(pw))
        yield
    while blk < c:
        e = jnp.where((rx >= blk) & (rx < 2 * blk), a, 0.0)
        ps = _split2(p)
        ep = _dot3s(_split2(e), ps)
        yield
        p = p - _dot3s(ps, _split2(ep))
        yield
        blk *= 2
    return p


def _shift_rows(x, prev, j, pos):
    rp = pltpu.roll(prev, j, axis=0)
    if prev.shape[0] != x.shape[0]:
        rp = jnp.broadcast_to(rp[None], (x.shape[0] // 8, 8, x.shape[1])).reshape(x.shape)
    return jnp.where(pos < j, rp, pltpu.roll(x, j, axis=0))


def _conv_silu(x, xprev, cw, seg):
    pos = lax.broadcasted_iota(jnp.int32, x.shape, 0) & (seg - 1)
    acc = x * cw[3:4, :]
    for j in range(1, 4):
        acc = acc + _shift_rows(x, xprev, j, pos) * cw[3 - j:4 - j, :]
    return acc * _sigmoid(acc)


def _delta_tile_prep(q_raw, k_raw, v_raw, pq, pk, pv, cwq, cwk, cwv, beta_col, g_col, c, seg):
    rt = q_raw.shape[0]
    q = _conv_silu(q_raw, pq, cwq, seg)
    k = _conv_silu(k_raw, pk, cwk, seg)
    v = _conv_silu(v_raw, pv, cwv, seg)
    q = q * (lax.rsqrt(jnp.sum(q * q, axis=-1, keepdims=True) + EPS) * (A_DK ** -0.5))
    k = k * lax.rsqrt(jnp.sum(k * k, axis=-1, keepdims=True) + EPS)
    yield

    r_i = lax.broadcasted_iota(jnp.int32, (rt, rt), 0)
    c_i = lax.broadcasted_iota(jnp.int32, (rt, rt), 1)
    rx = r_i ^ c_i
    same = rx < c
    lower = same & (c_i <= r_i)
    strict = same & (c_i < r_i)

    g1 = g_col.astype(BF16)
    r1 = g_col - g1.astype(F32)
    g2 = r1.astype(BF16)
    g3 = (r1 - g2.astype(F32)).astype(BF16)
    lmat = lower.astype(BF16)
    umat = (same & (r_i <= c_i)).astype(BF16)
    ones = jnp.ones((rt, rt), BF16)
    gc = None
    gr = None
    for gp in (g1, g2, g3):
        a = _dot(lmat, jnp.broadcast_to(gp, (rt, LANES)))
        b = _dot(ones, jnp.broadcast_to(gp, (rt, rt)) * umat)
        gc = a if gc is None else gc + a
        gr = b if gr is None else gr + b
    gc_full = jnp.concatenate([gc] * (rt // LANES), axis=1)
    decay = jnp.where(lower, jnp.exp(jnp.where(lower, gc_full - gr, 0.0)), 0.0)
    yield

    kb = k.astype(BF16)
    kk = _dot_nt(kb, kb)
    a_mat = jnp.where(strict, beta_col * kk * decay, 0.0)
    yield
    t_inv = yield from _unit_lower_inverse(a_mat, rx, c)
    rhs = jnp.concatenate([beta_col * v, (beta_col * jnp.exp(gc)) * k], axis=1)
    sol = _dot3s(_split2(t_inv), _split2(rhs))
    yield
    u_base = sol[:, :LANES]
    w_mat = sol[:, LANES:]
    qk = jnp.where(lower, _dot_nt(q.astype(BF16), kb) * decay, 0.0)
    q_dec = q * jnp.exp(gc)
    yield
    return k, gc, u_base, w_mat, qk, q_dec


def _gated_out_norm(o, z, og):
    return _rms(o, og) * (z * _sigmoid(z))


def _head_cols(ba, alog_row, dtb_row, h, n_heads):
    lane = lax.broadcasted_iota(jnp.int32, ba.shape, 1)
    sig = _sigmoid(ba)
    g_all = -jnp.exp(alog_row) * _softplus(ba + dtb_row)
    beta_col = jnp.sum(jnp.where(lane == h, sig, 0.0), axis=1, keepdims=True)
    g_col = jnp.sum(jnp.where(lane == n_heads + h, g_all, 0.0), axis=1, keepdims=True)
    return beta_col, g_col


def _delta_prompt_kernel(q_ref, k_ref, v_ref, z_ref, ba_ref, cwq_ref, cwk_ref, cwv_ref, alog_ref, dtb_ref, og_ref,
                         y_ref, sfin_ref, xprev_ref, s_ref, *, n_heads, chunk):
    j = pl.program_id(2)
    hp = pl.program_id(1)
    rt = q_ref.shape[0]

    @pl.when(j == 0)
    def _():
        xprev_ref[...] = jnp.zeros_like(xprev_ref)
        s_ref[...] = jnp.zeros_like(s_ref)

    ba = ba_ref[...]

    def head(hh):
        cs = slice(hh * LANES, (hh + 1) * LANES)
        beta_col, g_col = _head_cols(ba, alog_ref[...], dtb_ref[...], hp * DELTA_HP + hh, n_heads)
        q_raw, k_raw, v_raw = q_ref[:, cs], k_ref[:, cs], v_ref[:, cs]
        k, gc, u_base, w_mat, qk, q_dec = yield from _delta_tile_prep(
            q_raw, k_raw, v_raw, xprev_ref[0, :, cs], xprev_ref[1, :, cs], xprev_ref[2, :, cs],
            cwq_ref[:, cs], cwk_ref[:, cs], cwv_ref[:, cs], beta_col, g_col, chunk, rt)
        xprev_ref[0, :, cs] = q_raw[rt - 8:]
        xprev_ref[1, :, cs] = k_raw[rt - 8:]
        xprev_ref[2, :, cs] = v_raw[rt - 8:]
        s = s_ref[hh]
        us, os_ = [], []
        for cc in range(rt // chunk):
            rs = slice(cc * chunk, (cc + 1) * chunk)
            sb = s.astype(BF16)
            u_c = u_base[rs] - _dot(w_mat[rs].astype(BF16), sb)
            os_.append(_dot(q_dec[rs].astype(BF16), sb))
            yield
            gl = gc[(cc + 1) * chunk - 1:(cc + 1) * chunk, :]
            k_dec = k[rs] * jnp.exp(gl - gc[rs])
            s = jnp.exp(gl) * s + _dot_tn(k_dec.astype(BF16), u_c.astype(BF16))
            us.append(u_c)
            yield
        s_ref[hh] = s
        u_all = jnp.concatenate(us, axis=0)
        o = jnp.concatenate(os_, axis=0) + _dot(qk.astype(BF16), u_all.astype(BF16))
        yield
        y_ref[:, cs] = _gated_out_norm(o, z_ref[:, cs], og_ref[...]).astype(y_ref.dtype)

    _lockstep([head(hh) for hh in range(DELTA_HP)])

    @pl.when(j == pl.num_programs(2) - 1)
    def _():
        sfin_ref[0] = s_ref[...]


def _delta_prompt(proj_a, proj_ba, conv_w, alog_row, dtb_row, out_g, *, n_seq, seq_len, n_heads, chunk):
    rt, hp = DELTA_RT, DELTA_HP
    assert seq_len % rt == 0 and rt % chunk == 0 and n_heads % hp == 0
    tiles = seq_len // rt
    nhp = n_heads // hp
    wblk = hp * LANES

    def col_spec(base):
        return pl.BlockSpec((rt, wblk), lambda n, h, j: (n * tiles + j, base + h))

    def cw_spec(base):
        return pl.BlockSpec((4, wblk), lambda n, h, j: (0, base + h))

    row128 = pl.BlockSpec((1, LANES), lambda n, h, j: (0, 0))
    kern = functools.partial(_delta_prompt_kernel, n_heads=n_heads, chunk=chunk)
    return pl.pallas_call(
        kern, grid=(n_seq, nhp, tiles),
        in_specs=[col_spec(0), col_spec(nhp), col_spec(2 * nhp), col_spec(3 * nhp),
                  pl.BlockSpec((rt, LANES), lambda n, h, j: (n * tiles + j, 0)),
                  cw_spec(0), cw_spec(nhp), cw_spec(2 * nhp), row128, row128, row128],
        out_specs=[pl.BlockSpec((rt, wblk), lambda n, h, j: (n * tiles + j, h)),
                   pl.BlockSpec((1, hp, LANES, LANES), lambda n, h, j: (n, h, 0, 0))],
        out_shape=[jax.ShapeDtypeStruct((n_seq * seq_len, n_heads * LANES), BF16),
                   jax.ShapeDtypeStruct((n_seq, n_heads, LANES, LANES), F32)],
        scratch_shapes=[pltpu.VMEM((3, 8, wblk), F32), pltpu.VMEM((hp, LANES, LANES), F32)],
        compiler_params=_params(3), name="delta_prompt",
    )(proj_a, proj_a, proj_a, proj_a, proj_ba, conv_w, conv_w, conv_w, alog_row, dtb_row, out_g)


def _attn_prompt_kernel(q_ref, kc_ref, kp_ref, vc_ref, vp_ref, sk_ref, o_ref, *, blocks_per_seq):
    b = pl.program_id(0) % blocks_per_seq
    w = q_ref.shape[0]
    rows = B_GROUP * w
    r_i = lax.broadcasted_iota(jnp.int32, (rows, 2 * w), 0) & (w - 1)
    c_i = lax.broadcasted_iota(jnp.int32, (rows, 2 * w), 1)
    mask = (c_i > r_i) & (c_i <= r_i + w) & ((b > 0) | (c_i >= w))
    grp = lax.shift_right_logical(lax.broadcasted_iota(jnp.int32, (rows, 1), 0), int(math.log2(w)))
    scale = B_HD ** -0.5
    def kv_head(kvh):
        ks = slice(kvh * B_HD, (kvh + 1) * B_HD)
        kk = jnp.concatenate([kp_ref[:, ks], kc_ref[:, ks]], axis=0).astype(BF16)
        vv = jnp.concatenate([vp_ref[:, ks], vc_ref[:, ks]], axis=0).astype(BF16)
        h0 = kvh * B_GROUP
        q = jnp.concatenate([q_ref[:, (h0 + g) * B_HD:(h0 + g + 1) * B_HD] for g in range(B_GROUP)], axis=0)
        s = jnp.where(mask, _dot_nt(q.astype(BF16), kk) * scale, -jnp.inf)
        yield
        sink = jnp.zeros((rows, 1), F32)
        for g in range(B_GROUP):
            sink = jnp.where(grp == g, sk_ref[h0 + g], sink)
        m = jnp.maximum(jnp.max(s, axis=-1, keepdims=True), sink)
        e = jnp.exp(s - m)
        p = e / (jnp.sum(e, axis=-1, keepdims=True) + jnp.exp(sink - m))
        yield
        o = _dot(p.astype(BF16), vv)
        yield
        for g in range(B_GROUP):
            o_ref[:, (h0 + g) * B_HD:(h0 + g + 1) * B_HD] = o[g * w:(g + 1) * w].astype(o_ref.dtype)

    _lockstep([kv_head(kvh) for kvh in range(B_KV_HEADS)])


def _attn_prompt(q_arr, kv_arr, sinks, *, n_seq, seq_len):
    w = WINDOW
    bps = seq_len // w
    cur = lambda r: (r, 0)
    return pl.pallas_call(
        functools.partial(_attn_prompt_kernel, blocks_per_seq=bps), grid=(n_seq * bps,),
        in_specs=[pl.BlockSpec((w, B_Q), cur),
                  pl.BlockSpec((w, B_KV), cur), pl.BlockSpec((w, B_KV), lambda r: (jnp.maximum(r - 1, 0), 0)),
                  pl.BlockSpec((w, B_KV), lambda r: (r, 1)), pl.BlockSpec((w, B_KV), lambda r: (jnp.maximum(r - 1, 0), 1)),
                  pl.BlockSpec(memory_space=pltpu.SMEM)],
        out_specs=pl.BlockSpec((w, B_Q), cur),
        out_shape=jax.ShapeDtypeStruct((n_seq * seq_len, B_Q), BF16),
        compiler_params=_params(1), name="attn_prompt",
    )(q_arr, kv_arr, kv_arr, kv_arr, kv_arr, sinks)


def _cmlp_kernel(u_ref, v_ref, lg_ref, lb_ref, ws_ref, bs_ref, o_ref, *vn_ref):
    v = _gelu_tanh(v_ref[...])
    mu = jnp.mean(v, axis=-1, keepdims=True)
    d = v - mu
    var = jnp.mean(d * d, axis=-1, keepdims=True)
    vn = d * lax.rsqrt(var + EPS) * lg_ref[...] + lb_ref[...]
    if vn_ref:
        vn_ref[0][...] = vn
    c = v.shape[0]
    tril = lax.broadcasted_iota(jnp.int32, (c, c), 1) <= lax.broadcasted_iota(jnp.int32, (c, c), 0)
    for g in range(C_GROUPS):
        gs = slice(g * C_GC, (g + 1) * C_GC)
        wc = jnp.where(tril, ws_ref[g], 0.0).astype(BF16)
        mixed = _dot(wc, vn[:, gs].astype(BF16)) + bs_ref[:, g:g + 1]
        o_ref[:, gs] = (_gelu_tanh(u_ref[:, gs]) * mixed).astype(o_ref.dtype)


def _cmlp(uv_arr, ln_g, ln_b, w_s, b_s_t, *, row0, n_rows, chunk, want_vn):
    assert row0 % chunk == 0 and n_rows % chunk == 0
    rb0 = row0 // chunk
    blk = pl.BlockSpec((chunk, C_HALF), lambda r: (r, 0))
    vec = pl.BlockSpec((1, C_HALF), lambda r: (0, 0))
    out_specs = [blk, blk] if want_vn else blk
    y_shape = jax.ShapeDtypeStruct((n_rows, C_HALF), BF16)
    out_shape = [y_shape, jax.ShapeDtypeStruct((n_rows, C_HALF), F32)] if want_vn else y_shape
    return pl.pallas_call(
        _cmlp_kernel, grid=(n_rows // chunk,),
        in_specs=[pl.BlockSpec((chunk, C_HALF), lambda r: (rb0 + r, 0)),
                  pl.BlockSpec((chunk, C_HALF), lambda r: (rb0 + r, 1)), vec, vec,
                  pl.BlockSpec((C_GROUPS, chunk, chunk), lambda r: (0, 0, 0)),
                  pl.BlockSpec((chunk, C_GROUPS), lambda r: (0, 0))],
        out_specs=out_specs, out_shape=out_shape, compiler_params=_params(1), name="cmlp",
    )(uv_arr, uv_arr, ln_g, ln_b, w_s, b_s_t)


def _delta_sample_kernel(q_ref, k_ref, v_ref, z_ref, ba_ref, pq_ref, pk_ref, pv_ref, cwq_ref, cwk_ref, cwv_ref,
                         alog_ref, dtb_ref, og_ref, s0_ref, y_ref, snew_ref,
                         ub_ref, wm_ref, qd_ref, kd_ref, eg_ref, u_ref, oi_ref, *, n_heads, seq):
    h = pl.program_id(0)
    rt = q_ref.shape[0]
    beta_col, g_col = _head_cols(ba_ref[...], alog_ref[...], dtb_ref[...], h, n_heads)
    k, gc, u_base, w_mat, qk, q_dec = _run(_delta_tile_prep(
        q_ref[...], k_ref[...], v_ref[...], pq_ref[...], pk_ref[...], pv_ref[...],
        cwq_ref[...], cwk_ref[...], cwv_ref[...], beta_col, g_col, seq, seq))
    r_i = lax.broadcasted_iota(jnp.int32, (rt, rt), 0)
    c_i = lax.broadcasted_iota(jnp.int32, (rt, rt), 1)
    sel = (c_i == (r_i | (seq - 1))).astype(BF16)
    g1 = gc.astype(BF16)
    r1 = gc - g1.astype(F32)
    g2 = r1.astype(BF16)
    g3 = (r1 - g2.astype(F32)).astype(BF16)
    gl = _dot(sel, g1) + (_dot(sel, g2) + _dot(sel, g3))
    pad = jnp.zeros((8, LANES), F32)
    for ref, val in ((ub_ref, u_base), (wm_ref, w_mat), (qd_ref, q_dec), (kd_ref, k * jnp.exp(gl - gc)),
                     (eg_ref, jnp.exp(gl))):
        ref[0:rt, :] = val
        ref[rt:rt + 8, :] = pad

    def one_sequence(s):
        r0 = pl.multiple_of(s * seq, seq)
        win = pl.ds(r0, 2 * seq)
        s0 = s0_ref[s]
        sb = s0.astype(BF16)
        first = lax.broadcasted_iota(jnp.int32, (2 * seq, LANES), 0) < seq
        u16 = ub_ref[win, :] - _dot(wm_ref[win, :].astype(BF16), sb)
        o16 = _dot(qd_ref[win, :].astype(BF16), sb)
        yield
        kd16 = jnp.where(first, kd_ref[win, :], 0.0)
        snew_ref[s] = eg_ref[pl.ds(r0, 1), :] * s0 + _dot_tn(kd16.astype(BF16), jnp.where(first, u16, 0.0).astype(BF16))
        u_ref[pl.ds(r0, seq), :] = u16[0:seq]
        oi_ref[pl.ds(r0, seq), :] = o16[0:seq]

    def body(i, carry):
        _lockstep([one_sequence(SAMPLE_LOCKSTEP * i + d) for d in range(SAMPLE_LOCKSTEP)])
        return carry

    lax.fori_loop(0, rt // seq // SAMPLE_LOCKSTEP, body, 0)
    o = oi_ref[...] + _dot(qk.astype(BF16), u_ref[...].astype(BF16))
    y_ref[...] = _gated_out_norm(o, z_ref[...], og_ref[...]).astype(y_ref.dtype)


def _delta_sample(proj_a, proj_ba, p_conv, conv_w, alog_row, dtb_row, out_g, s0, *, row0, n_seq, seq, n_heads):
    rt = n_seq * seq
    assert seq == 8 and row0 % rt == 0 and rt % LANES == 0
    rb = row0 // rt

    def col_spec(base):
        return pl.BlockSpec((rt, LANES), lambda h: (rb, base + h))

    def p_spec(base):
        return pl.BlockSpec((rt, LANES), lambda h: (0, base + h))

    def cw_spec(base):
        return pl.BlockSpec((4, LANES), lambda h: (0, base + h))

    row128 = pl.BlockSpec((1, LANES), lambda h: (0, 0))
    st_spec = pl.BlockSpec((n_seq, None, LANES, LANES), lambda h: (0, h, 0, 0))
    big = pltpu.VMEM((rt + 8, LANES), F32)
    kern = functools.partial(_delta_sample_kernel, n_heads=n_heads, seq=seq)
    return pl.pallas_call(
        kern, grid=(n_heads,),
        in_specs=[col_spec(0), col_spec(n_heads), col_spec(2 * n_heads), col_spec(3 * n_heads),
                  pl.BlockSpec((rt, LANES), lambda h: (rb, 0)),
                  p_spec(0), p_spec(n_heads), p_spec(2 * n_heads),
                  cw_spec(0), cw_spec(n_heads), cw_spec(2 * n_heads), row128, row128, row128, st_spec],
        out_specs=[pl.BlockSpec((rt, LANES), lambda h: (0, h)), st_spec],
        out_shape=[jax.ShapeDtypeStruct((rt, n_heads * LANES), BF16),
                   jax.ShapeDtypeStruct((n_seq, n_heads, LANES, LANES), F32)],
        scratch_shapes=[big, big, big, big, big, pltpu.VMEM((rt, LANES), F32), pltpu.VMEM((rt, LANES), F32)],
        compiler_params=_params(1), name="delta_sample",
    )(proj_a, proj_a, proj_a, proj_a, proj_ba, p_conv, p_conv, p_conv, conv_w, conv_w, conv_w,
      alog_row, dtb_row, out_g, s0)


def _attn_sample_kernel(q_ref, kvn_ref, kb_ref, vb_ref, sk_ref, o_ref):
    lq = q_ref.shape[0]
    wb = kb_ref.shape[0]
    nk = 2 * wb
    rows = B_GROUP * lq
    r_i = lax.broadcasted_iota(jnp.int32, (rows, nk), 0)
    c_i = lax.broadcasted_iota(jnp.int32, (rows, nk), 1)
    i = r_i & (lq - 1)
    mask = (c_i <= i + wb) & (c_i > i + wb - WINDOW)
    grp = lax.shift_right_logical(lax.broadcasted_iota(jnp.int32, (rows, 1), 0), int(math.log2(lq)))
    scale = B_HD ** -0.5
    zpad = jnp.zeros((nk - wb - lq, B_HD), F32)
    def kv_head(kvh):
        ks = slice(kvh * B_HD, (kvh + 1) * B_HD)
        vs = slice(B_KV + kvh * B_HD, B_KV + (kvh + 1) * B_HD)
        kk = jnp.concatenate([kb_ref[:, ks], kvn_ref[:, ks], zpad], axis=0).astype(BF16)
        vv = jnp.concatenate([vb_ref[:, ks], kvn_ref[:, vs], zpad], axis=0).astype(BF16)
        h0 = kvh * B_GROUP
        q = jnp.concatenate([q_ref[:, (h0 + g) * B_HD:(h0 + g + 1) * B_HD] for g in range(B_GROUP)], axis=0)
        s = jnp.where(mask, _dot_nt(q.astype(BF16), kk) * scale, -jnp.inf)
        yield
        sink = jnp.zeros((rows, 1), F32)
        for g in range(B_GROUP):
            sink = jnp.where(grp == g, sk_ref[h0 + g], sink)
        m = jnp.maximum(jnp.max(s, axis=-1, keepdims=True), sink)
        e = jnp.exp(s - m)
        p = e / (jnp.sum(e, axis=-1, keepdims=True) + jnp.exp(sink - m))
        yield
        o = _dot(p.astype(BF16), vv)
        yield
        for g in range(B_GROUP):
            o_ref[:, (h0 + g) * B_HD:(h0 + g + 1) * B_HD] = o[g * lq:(g + 1) * lq]

    _lockstep([kv_head(kvh) for kvh in range(B_KV_HEADS)])


def _attn_sample(q_arr, kv_arr, k_buf, v_buf, sinks, *, row0, n_seq, seq):
    assert row0 % seq == 0 and seq & (seq - 1) == 0 and k_buf.shape[1] >= seq
    rb0 = row0 // seq
    wb = k_buf.shape[1]
    buf = pl.BlockSpec((None, wb, B_KV), lambda s: (s, 0, 0))
    return pl.pallas_call(
        _attn_sample_kernel, grid=(n_seq,),
        in_specs=[pl.BlockSpec((seq, B_Q), lambda s: (rb0 + s, 0)), pl.BlockSpec((seq, 2 * B_KV), lambda s: (rb0 + s, 0)),
                  buf, buf, pl.BlockSpec(memory_space=pltpu.SMEM)],
        out_specs=pl.BlockSpec((seq, B_Q), lambda s: (s, 0)),
        out_shape=jax.ShapeDtypeStruct((n_seq * seq, B_Q), F32),
        compiler_params=_params(1), name="attn_sample",
    )(q_arr, kv_arr, k_buf, v_buf, sinks)


def _merge_kernel(ya_ref, yb_ref, yc_ref, wa_ref, wb_ref, wc_ref, ga_ref, gb_ref, gc_ref, o_ref,
                  wab_ref, wbb_ref, wcb_ref):
    @pl.when(pl.program_id(1) == 0)
    def _():
        wab_ref[...] = wa_ref[...].astype(BF16)
        wbb_ref[...] = wb_ref[...].astype(BF16)
        wcb_ref[...] = wc_ref[...].astype(BF16)

    acc = _sigmoid(ga_ref[...]) * _dot(ya_ref[...], wab_ref[...])
    acc = acc + _sigmoid(gb_ref[...]) * _dot(yb_ref[...], wbb_ref[...])
    acc = acc + _sigmoid(gc_ref[...]) * _dot(yc_ref[...], wcb_ref[...])
    o_ref[...] = acc.astype(o_ref.dtype)


def _gated_merge(y_a, y_b, y_c, w_a, w_b, w_c, layer, gate_logits, *, bm, bn):
    t, kdim = y_a.shape
    d = w_a.shape[2]
    assert t % bm == 0 and d % bn == 0
    nj = d // bn
    y_spec = pl.BlockSpec((bm, kdim), lambda j, i: (i, 0))
    w_spec = pl.BlockSpec((None, kdim, bn), lambda j, i: (layer, 0, j))

    def g_spec(b):
        return pl.BlockSpec((bm, bn), lambda j, i: (i, b * nj + j))

    return pl.pallas_call(
        _merge_kernel, grid=(nj, t // bm),
        in_specs=[y_spec, y_spec, y_spec, w_spec, w_spec, w_spec, g_spec(0), g_spec(1), g_spec(2)],
        out_specs=pl.BlockSpec((bm, bn), lambda j, i: (i, j)),
        out_shape=jax.ShapeDtypeStruct((t, d), BF16),
        scratch_shapes=[pltpu.VMEM((kdim, bn), BF16)] * 3,
        compiler_params=_params(2), name="gated_merge",
    )(y_a, y_b, y_c, w_a, w_b, w_c, gate_logits, gate_logits, gate_logits)


def _conv3(h, hprev, cw, b, pos):
    acc = h * cw[2:3, :] + b
    for j in (1, 2):
        acc = acc + _shift_rows(h, hprev, j, pos) * cw[2 - j:3 - j, :]
    return acc


def _ffn_up_kernel(x_ref, wg_ref, wu_ref, cwg_ref, cwu_ref, bg_ref, bu_ref, pg_ref, pu_ref,
                   act_ref, tg_ref, tu_ref, hsg_ref, hsu_ref, wgb_ref, wub_ref, prevg_ref, prevu_ref,
                   *, tiles_per_seq, n_prompt_tiles, sample_rows, sample_len):
    m = pl.program_id(1)
    bm = x_ref.shape[0]

    @pl.when(m == 0)
    def _():
        wgb_ref[...] = wg_ref[...].astype(BF16)
        wub_ref[...] = wu_ref[...].astype(BF16)

    @pl.when(m < n_prompt_tiles)
    def _():
        x = x_ref[...]
        hg = _dot(x, wgb_ref[...])
        hu = _dot(x, wub_ref[...])
        first = (m % tiles_per_seq) == 0
        pos = lax.broadcasted_iota(jnp.int32, hg.shape, 0)
        pg = jnp.where(first, 0.0, prevg_ref[...])
        pu = jnp.where(first, 0.0, prevu_ref[...])
        g = _conv3(hg, pg, cwg_ref[...], bg_ref[...], pos)
        u = _conv3(hu, pu, cwu_ref[...], bu_ref[...], pos)
        act_ref[...] = (_gelu_tanh(g) * u).astype(act_ref.dtype)
        prevg_ref[...] = hg
        prevu_ref[...] = hu
        tg_ref[...] = hg[bm - 8:, :]
        tu_ref[...] = hu[bm - 8:, :]

    @pl.when(m == n_prompt_tiles)
    def _():
        x = x_ref[0:sample_rows, :]
        hg = _dot(x, wgb_ref[...])
        hu = _dot(x, wub_ref[...])
        pos = lax.broadcasted_iota(jnp.int32, hg.shape, 0) & (sample_len - 1)
        g = _conv3(hg, pg_ref[...], cwg_ref[...], bg_ref[...], pos)
        u = _conv3(hu, pu_ref[...], cwu_ref[...], bu_ref[...], pos)
        act_ref[0:sample_rows, :] = (_gelu_tanh(g) * u).astype(act_ref.dtype)
        hsg_ref[...] = hg
        hsu_ref[...] = hu


def _ffn_up(x, w_up, layer, conv_w, conv_b, p_state, *, n_seq, seq_len, sample_rows, sample_len, bm, bn):
    t, d = x.shape
    f = w_up.shape[2] // 2
    tp = n_seq * seq_len
    assert seq_len % bm == 0 and f % bn == 0 and t == tp + sample_rows and sample_rows <= bm
    npt = tp // bm
    nj = f // bn
    kern = functools.partial(_ffn_up_kernel, tiles_per_seq=seq_len // bm, n_prompt_tiles=npt,
                             sample_rows=sample_rows, sample_len=sample_len)

    def half(rows, base):
        return pl.BlockSpec((rows, bn), lambda j, m: (0, base + j))

    def w_half(base):
        return pl.BlockSpec((None, d, bn), lambda j, m: (layer, 0, base + j))

    tail_spec = pl.BlockSpec((8, bn), lambda j, m: (jnp.minimum(m, npt - 1), j))
    return pl.pallas_call(
        kern, grid=(nj, npt + 1),
        in_specs=[pl.BlockSpec((bm, d), lambda j, m: (m, 0)),
                  w_half(0), w_half(nj), half(3, 0), half(3, nj), half(1, 0), half(1, nj),
                  half(sample_rows, 0), half(sample_rows, nj)],
        out_specs=[pl.BlockSpec((bm, bn), lambda j, m: (m, j)), tail_spec, tail_spec,
                   half(sample_rows, 0), half(sample_rows, 0)],
        out_shape=[jax.ShapeDtypeStruct((t, f), BF16),
                   jax.ShapeDtypeStruct((npt * 8, f), F32), jax.ShapeDtypeStruct((npt * 8, f), F32),
                   jax.ShapeDtypeStruct((sample_rows, f), F32), jax.ShapeDtypeStruct((sample_rows, f), F32)],
        scratch_shapes=[pltpu.VMEM((d, bn), BF16), pltpu.VMEM((d, bn), BF16),
                        pltpu.VMEM((bm, bn), F32), pltpu.VMEM((bm, bn), F32)],
        compiler_params=_params(2), name="ffn_up",
    )(x, w_up, w_up, conv_w, conv_w, conv_b, conv_b, p_state, p_state)


def _ple_kernel(x_ref, p_ref, wg_ref, wp_ref, h_ref, o_ref, wgb_ref, wpb_ref):
    @pl.when(pl.program_id(1) == 0)
    def _():
        wgb_ref[...] = wg_ref[...].astype(BF16)
        wpb_ref[...] = wp_ref[...].astype(BF16)

    gate = _sigmoid(_dot(x_ref[...], wgb_ref[...]))
    o_ref[...] = h_ref[...] + gate * _dot(p_ref[...], wpb_ref[...])


def _ple_update(hn, p, w_gate, w_ple, layer, h, *, bm, bn):
    t, d = h.shape
    kp = p.shape[1]
    assert t % bm == 0 and d % bn == 0
    blk = pl.BlockSpec((bm, bn), lambda j, i: (i, j))
    return pl.pallas_call(
        _ple_kernel, grid=(d // bn, t // bm),
        in_specs=[pl.BlockSpec((bm, d), lambda j, i: (i, 0)), pl.BlockSpec((bm, kp), lambda j, i: (i, 0)),
                  pl.BlockSpec((None, d, bn), lambda j, i: (layer, 0, j)),
                  pl.BlockSpec((None, kp, bn), lambda j, i: (layer, 0, j)), blk],
        out_specs=blk, out_shape=jax.ShapeDtypeStruct((t, d), F32),
        scratch_shapes=[pltpu.VMEM((d, bn), BF16), pltpu.VMEM((kp, bn), BF16)],
        input_output_aliases={4: 0}, compiler_params=_params(2), name="ple_update",
    )(hn, p, w_gate, w_ple, h)


def _seam_rows(state, seq):
    n_seq, w, c = state.shape
    padded = jnp.pad(state, ((0, 0), (seq - w, 0), (0, 0)))
    return jnp.roll(padded, -1, axis=0).reshape(n_seq * seq, c)
def kernel(x_prompt, x_sample, p_prompt, p_sample, state_a_conv, state_delta, cache_win_k, cache_win_v, state_ffn_conv, norm_mix_pre, norm_mix_post, norm_ffn_pre, norm_ffn_post, norm_ple, w_in, a_conv_w, a_log, a_dt_bias, a_out_norm, b_sinks, c_ln_g, c_ln_b, c_w_s, c_b_s, w_br_a, w_br_b, w_br_c, w_o, w_up, ffn_conv_w, ffn_conv_b, w_down, w_ple, w_ple_gate):
    nb, ls = x_prompt.shape[:2]
    ns, lq = x_sample.shape[:2]
    tp, ts = nb * ls, ns * lq
    ple = p_prompt.shape[-1]
    h = jnp.concatenate([x_prompt.reshape(tp, D_MODEL), x_sample.reshape(ts, D_MODEL)], axis=0)
    w_in_bf = w_in.astype(BF16)
    shifted = functools.partial(_matmul_shifted, bm=BM, bn=BN, shift=SHIFT)
    st_p, st_s = [], []
    for i in range(DEPTH):
        xn = _rms_norm_bf16(h, norm_mix_pre[i])
        proj_a = _matmul(xn, w_in_bf, i, bm=BM, bn=BN_WIDE, n_cols=OFF_BA)
        proj_ba = _matmul(xn, w_in_bf, i, bm=BM, bn=LANES, col0=OFF_BA, n_cols=LANES)
        proj_q = shifted(xn, w_in_bf, i, col0=OFF_BQ, n_cols=B_Q)
        proj_kv = shifted(xn, w_in_bf, i, col0=OFF_KV, n_cols=2 * B_KV)
        proj_uv = shifted(xn, w_in_bf, i, col0=OFF_CUV, n_cols=2 * C_HALF)
        proj_g = shifted(xn, w_in_bf, i, col0=OFF_GATES, n_cols=N_BRANCH * D_MODEL)
        alog_row = jnp.zeros((1, LANES), F32).at[0, A_HEADS:2 * A_HEADS].set(a_log[i])
        dtb_row = jnp.zeros((1, LANES), F32).at[0, A_HEADS:2 * A_HEADS].set(a_dt_bias[i])
        ya_p, s_p = _delta_prompt(proj_a, proj_ba, a_conv_w[i], alog_row, dtb_row, a_out_norm[i].reshape(1, A_DV),
                                  n_seq=nb, seq_len=ls, n_heads=A_HEADS, chunk=math.gcd(ls, DELTA_CHUNK))
        abuf_p = jnp.stack([proj_a[(n + 1) * ls - (A_CONV - 1):(n + 1) * ls, :A_CONV_CH] for n in range(nb)])
        ya_s, s_s = _delta_sample(proj_a, proj_ba, _seam_rows(state_a_conv[i], lq), a_conv_w[i], alog_row, dtb_row,
                                  a_out_norm[i].reshape(1, A_DV), state_delta[i],
                                  row0=tp, n_seq=ns, seq=lq, n_heads=A_HEADS)
        abuf_s = proj_a[tp:, :A_CONV_CH].reshape(ns, lq, A_CONV_CH)[:, lq - (A_CONV - 1):]
        yb_p = _attn_prompt(proj_q, proj_kv, b_sinks[i], n_seq=nb, seq_len=ls)
        wb = min(WINDOW, ls)
        kv_tail = proj_kv[:tp].reshape(nb, ls, 2, B_KV_HEADS, B_HD)[:, ls - wb:]
        k_p, v_p = kv_tail[:, :, 0], kv_tail[:, :, 1]
        wbs = cache_win_k.shape[2]
        yb_s = _attn_sample(proj_q, proj_kv, cache_win_k[i].reshape(ns, wbs, B_KV), cache_win_v[i].reshape(ns, wbs, B_KV),
                            b_sinks[i], row0=tp, n_seq=ns, seq=lq)
        kv_s = proj_kv[tp:].reshape(ns, lq, 2, B_KV_HEADS, B_HD)
        k_s = jnp.concatenate([cache_win_k[i], kv_s[:, :, 0]], axis=1)[:, lq:]
        v_s = jnp.concatenate([cache_win_v[i], kv_s[:, :, 1]], axis=1)[:, lq:]
        ln_g, ln_b = c_ln_g[i].reshape(1, C_HALF), c_ln_b[i].reshape(1, C_HALF)
        yc_p = _cmlp(proj_uv, ln_g, ln_b, c_w_s[i], c_b_s[i].T, row0=0, n_rows=tp, chunk=C_CHUNK, want_vn=False)
        eye_s = jnp.eye(ns, dtype=F32)
        w_s_blk = jax.vmap(lambda w: jnp.kron(eye_s, w))(c_w_s[i][:, :lq, :lq])
        yc_s, cv_s = _cmlp(proj_uv, ln_g, ln_b, w_s_blk, jnp.tile(c_b_s[i].T[:lq], (ns, 1)),
                           row0=tp, n_rows=ts, chunk=ts, want_vn=True)
        cv_s = cv_s.reshape(ns, lq, C_HALF)
        y_a = jnp.concatenate([ya_p, ya_s])
        y_b = jnp.concatenate([yb_p, yb_s.astype(BF16)])
        y_c = jnp.concatenate([yc_p, yc_s])
        merged = _gated_merge(y_a, y_b, y_c, w_br_a, w_br_b, w_br_c, i, proj_g, bm=BM, bn=BN)
        h, xf = _residual_norm(h, _matmul(merged, w_o, i, bm=BM, bn=BN), norm_mix_post[i], norm_ffn_pre[i])
        act, tail_g, tail_u, hs_g, hs_u = _ffn_up(
            xf, w_up, i, ffn_conv_w[i], ffn_conv_b[i].reshape(1, 2 * D_FF), _seam_rows(state_ffn_conv[i], lq),
            n_seq=nb, seq_len=ls, sample_rows=ts, sample_len=lq, bm=FFN_BM, bn=FFN_BN)
        tails = jnp.concatenate([tail_g, tail_u], axis=1).reshape(nb, ls // FFN_BM, 8, 2 * D_FF)
        fb_p = tails[:, -1, 8 - (FFN_CONV - 1):]
        fb_s = jnp.concatenate([hs_g, hs_u], axis=1).reshape(ns, lq, 2 * D_FF)[:, lq - (FFN_CONV - 1):]
        f = None
        for c in range(D_FF // D_MODEL):
            f = _matmul(act, w_down, i, bm=BM, bn=BN, k_chunk=c, bk=D_MODEL, acc=f)
        h, hn = _residual_norm(h, f, norm_ffn_post[i], norm_ple[i])
        p_i = jnp.concatenate([p_prompt[i].reshape(tp, ple), p_sample[i].reshape(ts, ple)]).astype(BF16)
        h = _ple_update(hn, p_i, w_ple_gate, w_ple, i, h, bm=BM, bn=BN)
        st_p.append((abuf_p, s_p, k_p, v_p, fb_p))
        st_s.append((abuf_s, s_s, k_s, v_s, fb_s, cv_s))

    def stack(states, j):
        return jnp.stack([s[j] for s in states])

    y_prompt = h[:tp].reshape(nb, ls, D_MODEL)
    y_sample = h[tp:].reshape(ns, lq, D_MODEL)
    return (y_prompt, y_sample,
            stack(st_p, 0), stack(st_p, 1), stack(st_p, 2), stack(st_p, 3), stack(st_p, 4),
            stack(st_s, 0), stack(st_s, 1), stack(st_s, 2), stack(st_s, 3), stack(st_s, 4),
            stack(st_s, 5))
```
